```python
import math
import jax, jax.numpy as jnp
from jax import lax
import numpy as np

D_MODEL = 2048
BATCH = 16
SEQ = 2048
DEPTH = 4

GRID_W = 64
CTX_LEN = 256
EPS = 1e-6

MLA_HEADS = 8
QK_NOPE = 128
QK_ROPE = 64
V_HEAD = 128
QK_HEAD = QK_NOPE + QK_ROPE
Q_LORA = 512
KV_LORA = 256
MLA_WIDTH = MLA_HEADS * V_HEAD
ROPE_THETA = 10000.0
ROPE_AXIS_DIM = QK_ROPE // 2
ROPE_FREQS = ROPE_AXIS_DIM // 2
ATTN_SCALE = 1.0 / math.sqrt(QK_HEAD)
Q_BLOCK = 128

SGU_GROUPS = 8
SGU_CHUNK = 128
SGU_GROUP_CH = 128
SGU_WIDTH = SGU_GROUPS * SGU_GROUP_CH

POOL_WINDOWS = (2, 4, 8, 16)
POOL_GROUP_CH = 256
POOL_WIDTH = len(POOL_WINDOWS) * POOL_GROUP_CH

N_BRANCH = 3
IN_SPLITS = (Q_LORA, KV_LORA, QK_ROPE, 2 * SGU_WIDTH, POOL_WIDTH, N_BRANCH * D_MODEL)
IN_COLS = sum(IN_SPLITS)

N_GROUPS = 4
EXPERTS_PER_GROUP = 8
N_EXPERTS = N_GROUPS * EXPERTS_PER_GROUP
TOP_K = 2
EXPERT_HIDDEN = 512

kernel_name = "hybrid_mla_sgu_pool_hmoe_dit"


def rmsnorm(x, g=None):
    xf = x.astype(jnp.float32)
    y = xf * lax.rsqrt(jnp.mean(xf * xf, axis=-1, keepdims=True) + EPS)
    if g is not None:
        y = y * g.astype(jnp.float32)
    return y.astype(x.dtype)


def modulate(h, shift, scale):
    return h * (1 + scale) + shift


def axial_rope_tables(rows):
    row = jnp.repeat(jnp.arange(rows, dtype=jnp.float32), GRID_W)
    col = jnp.tile(jnp.arange(GRID_W, dtype=jnp.float32), rows)
    inv = ROPE_THETA ** (-jnp.arange(0, ROPE_AXIS_DIM, 2, dtype=jnp.float32) / ROPE_AXIS_DIM)
    ang = jnp.stack([row[:, None] * inv, col[:, None] * inv], axis=1)
    return jnp.cos(ang), jnp.sin(ang)


def apply_axial_rope(x, cos, sin):
    xr = x.reshape(x.shape[:-1] + (2, 2, ROPE_FREQS))
    x1, x2 = xr[..., 0, :], xr[..., 1, :]
    cos = cos[:, None].astype(x.dtype)
    sin = sin[:, None].astype(x.dtype)
    out = jnp.stack([x1 * cos - x2 * sin, x1 * sin + x2 * cos], axis=-2)
    return out.reshape(x.shape)


def split_cols(p):
    return jnp.split(p, [int(o) for o in np.cumsum(IN_SPLITS)[:-1]], axis=-1)


def mla_q(p_q, lp, rope):
    B, n, _ = p_q.shape
    q = (rmsnorm(p_q, lp["g_cq"]) @ lp["w_uq"]).reshape(B, n, MLA_HEADS, QK_HEAD)
    q = rmsnorm(q, lp["g_q"])
    if rope is not None:
        q = jnp.concatenate([q[..., :QK_NOPE], apply_axial_rope(q[..., QK_NOPE:], *rope)], axis=-1)
    return q


def mla_kv(p_kv, p_kr, lp, rope):
    B, n, _ = p_kv.shape
    kv = (rmsnorm(p_kv, lp["g_ckv"]) @ lp["w_ukv"]).reshape(B, n, MLA_HEADS, QK_NOPE + V_HEAD)
    k_nope, v = kv[..., :QK_NOPE], kv[..., QK_NOPE:]
    k_rope = jnp.broadcast_to(p_kr[:, :, None, :], (B, n, MLA_HEADS, QK_ROPE))
    k = rmsnorm(jnp.concatenate([k_nope, k_rope], axis=-1), lp["g_k"])
    if rope is not None:
        k = jnp.concatenate([k[..., :QK_NOPE], apply_axial_rope(k[..., QK_NOPE:], *rope)], axis=-1)
    return k, v


def attend(q, k, v):
    s = jnp.einsum('bqhd,bkhd->bhqk', q, k).astype(jnp.float32) * ATTN_SCALE
    p = jax.nn.softmax(s, axis=-1).astype(v.dtype)
    o = jnp.einsum('bhqk,bkhd->bqhd', p, v)
    return o.reshape(o.shape[0], o.shape[1], -1)


def latent_attention(q, k, v, kc, vc):
    B, n, H, dq = q.shape
    k_all = jnp.concatenate([k, kc], axis=1)
    v_all = jnp.concatenate([v, vc], axis=1)
    qb = q.reshape(B, n // Q_BLOCK, Q_BLOCK, H, dq).swapaxes(0, 1)
    o = lax.map(lambda qi: attend(qi, k_all, v_all), qb)
    return o.swapaxes(0, 1).reshape(B, n, MLA_WIDTH)


def spatial_gating(p, lp):
    B, n, _ = p.shape
    u, v = jnp.split(jax.nn.gelu(p), 2, axis=-1)
    v = rmsnorm(v, lp["g_sgu"])
    v = v.reshape(B, n // SGU_CHUNK, SGU_CHUNK, SGU_GROUPS, SGU_GROUP_CH)
    v = jnp.einsum('gqp,bnpgc->bnqgc', lp["w_sgu"], v) + lp["b_sgu"].T[:, :, None]
    return u * v.reshape(B, n, SGU_WIDTH)


def multiscale_pool(p, lp):
    B, n, _ = p.shape
    pf = p.astype(jnp.float32)
    cs = jnp.pad(jnp.cumsum(pf, axis=1), ((0, 0), (1, 0), (0, 0)))
    t = jnp.arange(n)
    outs = []
    for gi, w in enumerate(POOL_WINDOWS):
        lo = jnp.clip(t - w // 2, 0, n)
        hi = jnp.clip(t + w // 2, 0, n)
        sl = slice(gi * POOL_GROUP_CH, (gi + 1) * POOL_GROUP_CH)
        mean = (cs[:, hi, sl] - cs[:, lo, sl]) / (hi - lo).astype(jnp.float32)[:, None]
        outs.append(mean - pf[:, :, sl])
    m = jnp.stack(outs, axis=2).astype(p.dtype)
    y = jnp.einsum('bngc,gcd->bngd', m, lp["w_pool"]).reshape(B, n, POOL_WIDTH)
    return y * lp["s_pool"]


def merge_branches(a, b, cp, p_gate, lp):
    ga, gb, gc = jnp.split(jax.nn.sigmoid(p_gate), N_BRANCH, axis=-1)
    y = ga * (a @ lp["w_ao"]) + gb * (b @ lp["w_bo"]) + gc * (cp @ lp["w_co"])
    return y @ lp["w_out"]


def mixer_sublayer(h, hc, lp, rope, ctx_out):
    p_q, p_kv, p_kr, p_sgu, p_pool, p_gate = split_cols(h @ lp["w_in"])
    q = mla_q(p_q, lp, rope)
    k, v = mla_kv(p_kv, p_kr, lp, rope)
    if ctx_out:
        pc_q, pc_kv, pc_kr, pc_sgu, pc_pool, pc_gate = split_cols(hc @ lp["w_in"])
    else:
        pc_kv, pc_kr = jnp.split(hc @ lp["w_in"][:, Q_LORA:Q_LORA + KV_LORA + QK_ROPE], [KV_LORA], axis=-1)
    kc, vc = mla_kv(pc_kv, pc_kr, lp, None)
    a = latent_attention(q, k, v, kc, vc)
    y = merge_branches(a, spatial_gating(p_sgu, lp), multiscale_pool(p_pool, lp), p_gate, lp)
    yc = None
    if ctx_out:
        qc = mla_q(pc_q, lp, None)
        yc = merge_branches(attend(qc, kc, vc), spatial_gating(pc_sgu, lp),
                            multiscale_pool(pc_pool, lp), pc_gate, lp)
    return y, yc


def hier_moe(h, lp):
    lg = (h @ lp["w_rg"]).astype(jnp.float32) + lp["b_rg"]
    g_w, g_i = lax.top_k(jax.nn.softmax(lg, axis=-1), 1)
    le = jnp.einsum('td,gde->tge', h, lp["w_re"]).astype(jnp.float32) + lp["b_re"]
    le = jnp.take_along_axis(le, g_i[:, :, None], axis=1)[:, 0]
    e_w, e_i = lax.top_k(jax.nn.softmax(le, axis=-1), TOP_K)
    e_w = e_w / jnp.sum(e_w, axis=-1, keepdims=True)
    combine = g_w * e_w
    eid = g_i * EXPERTS_PER_GROUP + e_i
    gates = jnp.einsum('tk,tke->te', combine,
                       jax.nn.one_hot(eid, N_EXPERTS, dtype=jnp.float32)).astype(h.dtype)
    y = jnp.zeros_like(h)
    for e in range(N_EXPERTS):
        hid = jax.nn.silu(h @ lp["w_e_gate"][e]) * (h @ lp["w_e_up"][e])
        y = y + gates[:, e:e + 1] * (hid @ lp["w_e_down"][e])
    return y


def setup_inputs(seed: int = 0) -> dict:
    key = jax.random.key(seed)
    ks = jax.random.split(key, 29)
    f32 = jnp.float32
    L, D = DEPTH, D_MODEL

    def nrm(k, shape, scale):
        return jax.random.normal(k, shape, f32) * scale

    def gain(k, shape, s=0.05):
        return 1.0 + s * jax.random.normal(k, shape, f32)

    return {
        "x": nrm(ks[0], (BATCH, SEQ, D), 1.0),
        "c": nrm(ks[1], (BATCH, D), 1.0),
        "ctx": nrm(ks[2], (BATCH, CTX_LEN, D), 1.0),
        "c_ctx": nrm(ks[3], (D,), 1.0),
        "w_ada": nrm(ks[4], (L, D, 6 * D), 0.5 * D ** -0.5),
        "b_ada": nrm(ks[5], (L, 6 * D), 0.02),
        "w_in": nrm(ks[6], (L, D, IN_COLS), D ** -0.5),
        "g_cq": gain(ks[7], (L, Q_LORA)),
        "g_ckv": gain(ks[8], (L, KV_LORA)),
        "w_uq": nrm(ks[9], (L, Q_LORA, MLA_HEADS * QK_HEAD), Q_LORA ** -0.5),
        "w_ukv": nrm(ks[10], (L, KV_LORA, MLA_HEADS * (QK_NOPE + V_HEAD)), KV_LORA ** -0.5),
        "g_q": gain(ks[11], (L, QK_HEAD)),
        "g_k": gain(ks[12], (L, QK_HEAD)),
        "w_sgu": nrm(ks[13], (L, SGU_GROUPS, SGU_CHUNK, SGU_CHUNK), 0.5 * SGU_CHUNK ** -0.5),
        "b_sgu": gain(ks[14], (L, SGU_GROUPS, SGU_CHUNK), 0.1),
        "g_sgu": gain(ks[15], (L, SGU_WIDTH)),
        "w_pool": nrm(ks[16], (L, len(POOL_WINDOWS), POOL_GROUP_CH, POOL_GROUP_CH), POOL_GROUP_CH ** -0.5),
        "s_pool": gain(ks[17], (L, POOL_WIDTH)),
        "w_ao": nrm(ks[18], (L, MLA_WIDTH, D), MLA_WIDTH ** -0.5),
        "w_bo": nrm(ks[19], (L, SGU_WIDTH, D), SGU_WIDTH ** -0.5),
        "w_co": nrm(ks[20], (L, POOL_WIDTH, D), POOL_WIDTH ** -0.5),
        "w_out": nrm(ks[21], (L, D, D), D ** -0.5),
        "w_rg": nrm(ks[22], (L, D, N_GROUPS), D ** -0.5),
        "b_rg": nrm(ks[23], (L, N_GROUPS), 0.01),
        "w_re": nrm(ks[24], (L, N_GROUPS, D, EXPERTS_PER_GROUP), D ** -0.5),
        "b_re": nrm(ks[25], (L, N_GROUPS, EXPERTS_PER_GROUP), 0.01),
        "w_e_gate": nrm(ks[26], (L, N_EXPERTS, D, EXPERT_HIDDEN), D ** -0.5),
        "w_e_up": nrm(ks[27], (L, N_EXPERTS, D, EXPERT_HIDDEN), D ** -0.5),
        "w_e_down": nrm(ks[28], (L, N_EXPERTS, EXPERT_HIDDEN, D), EXPERT_HIDDEN ** -0.5),
    }


def reference(x, c, ctx, c_ctx, w_ada, b_ada, w_in, g_cq, g_ckv, w_uq, w_ukv, g_q, g_k,
              w_sgu, b_sgu, g_sgu, w_pool, s_pool, w_ao, w_bo, w_co, w_out,
              w_rg, b_rg, w_re, b_re, w_e_gate, w_e_up, w_e_down):
    B, n, _ = x.shape
    rows = n // GRID_W
    rope = axial_rope_tables(rows)
    xc = ctx
    for i in range(DEPTH):
        ctx_out = i < DEPTH - 1
        lp = {"w_in": w_in[i], "g_cq": g_cq[i], "g_ckv": g_ckv[i], "w_uq": w_uq[i], "w_ukv": w_ukv[i],
              "g_q": g_q[i], "g_k": g_k[i], "w_sgu": w_sgu[i], "b_sgu": b_sgu[i], "g_sgu": g_sgu[i],
              "w_pool": w_pool[i], "s_pool": s_pool[i], "w_ao": w_ao[i], "w_bo": w_bo[i], "w_co": w_co[i],
              "w_out": w_out[i], "w_rg": w_rg[i], "b_rg": b_rg[i], "w_re": w_re[i], "b_re": b_re[i],
              "w_e_gate": w_e_gate[i], "w_e_up": w_e_up[i], "w_e_down": w_e_down[i]}
        mod = jax.nn.silu(c) @ w_ada[i] + b_ada[i]
        mod_c = jax.nn.silu(c_ctx) @ w_ada[i] + b_ada[i]
        sh1, sc1, g1, sh2, sc2, g2 = jnp.split(mod[:, None, :], 6, axis=-1)
        csh1, csc1, cg1, csh2, csc2, cg2 = jnp.split(mod_c, 6, axis=-1)

        h = modulate(rmsnorm(x), sh1, sc1)
        hc = modulate(rmsnorm(xc), csh1, csc1)
        y, yc = mixer_sublayer(h, hc, lp, rope, ctx_out)
        x = x + g1 * y
        if ctx_out:
            xc = xc + cg1 * yc

        h = modulate(rmsnorm(x), sh2, sc2)
        if ctx_out:
            hc = modulate(rmsnorm(xc), csh2, csc2)
            tok = jnp.concatenate([h.reshape(-1, D_MODEL), hc.reshape(-1, D_MODEL)], axis=0)
            yt = hier_moe(tok, lp)
            n_lat = B * n
            x = x + g2 * yt[:n_lat].reshape(x.shape)
            xc = xc + cg2 * yt[n_lat:].reshape(xc.shape)
        else:
            x = x + g2 * hier_moe(h.reshape(-1, D_MODEL), lp).reshape(x.shape)
    return x
```

```python
import functools
import math

import jax
import jax.numpy as jnp
import numpy as np
from jax import lax
from jax.experimental import pallas as pl
from jax.experimental.pallas import tpu as pltpu

GRID_W = 64
EPS = 1e-6
MLA_HEADS = 8
QK_NOPE = 128
QK_ROPE = 64
V_HEAD = 128
QK_HEAD = QK_NOPE + QK_ROPE
Q_LORA = 512
KV_LORA = 256
MLA_WIDTH = MLA_HEADS * V_HEAD
ROPE_THETA = 10000.0
ROPE_AXIS_DIM = QK_ROPE // 2
ROPE_FREQS = ROPE_AXIS_DIM // 2
ATTN_SCALE = 1.0 / math.sqrt(QK_HEAD)
SGU_GROUPS = 8
SGU_CHUNK = 128
SGU_GROUP_CH = 128
SGU_WIDTH = SGU_GROUPS * SGU_GROUP_CH
POOL_WINDOWS = (2, 4, 8, 16)
POOL_GROUP_CH = 256
POOL_WIDTH = len(POOL_WINDOWS) * POOL_GROUP_CH
N_GROUPS = 4
EXPERTS_PER_GROUP = 8
N_EXPERTS = N_GROUPS * EXPERTS_PER_GROUP
EXPERT_HIDDEN = 512

LANES = 128
SUBLANES = 8
HEAD_PAD = 256
VMEM_LIMIT = 56 * 1024 * 1024

MLA_BLOCK = 1024
COL_SGU = 0
COL_POOL = 2 * SGU_WIDTH
COL_MLA = COL_POOL + POOL_WIDTH
COL_GATE = COL_MLA + MLA_BLOCK

POOL_HALO = 16
ROUTE_COLS = LANES
MOE_TILE = 256

_MXU = jnp.bfloat16
_ACT = jnp.bfloat16
F32 = jnp.float32


def _params(*sem):
    return pltpu.CompilerParams(dimension_semantics=sem, vmem_limit_bytes=VMEM_LIMIT)


def _tile(total, cap, mult=LANES):
    best = None
    for t in range(mult, min(total, cap) + 1, mult):
        if total % t == 0:
            best = t
    assert best is not None, (total, cap, mult)
    return best


def _dot(a, b):
    return jnp.dot(a.astype(_MXU), b.astype(_MXU), preferred_element_type=F32)


def _ada_kernel(c_ref, w_ref, b_ref, o_ref):
    c = c_ref[...]
    s = c * jax.nn.sigmoid(c)
    o_ref[0] = _dot(s, w_ref[0]) + b_ref[0]


def _ada(cc, w_ada, b_ada):
    L, D, N = w_ada.shape
    R = cc.shape[0]
    tn = _tile(N, 1024)
    return pl.pallas_call(
        _ada_kernel,
        out_shape=jax.ShapeDtypeStruct((L, R, N), F32),
        grid=(L, N // tn),
        in_specs=[
            pl.BlockSpec((R, D), lambda l, j: (0, 0)),
            pl.BlockSpec((1, D, tn), lambda l, j: (l, 0, j)),
            pl.BlockSpec((1, 1, tn), lambda l, j: (l, 0, j)),
        ],
        out_specs=pl.BlockSpec((1, R, tn), lambda l, j: (l, 0, j)),
        compiler_params=_params("parallel", "parallel"),
        name="ada_mod",
    )(cc, w_ada, b_ada.reshape(L, 1, N))


def _norm_mod(x, row0, n_lat, sh_b, sc_b, sh_c, sc_c):
    ms = jnp.mean(x * x, axis=-1, keepdims=True)
    xn = x * lax.rsqrt(ms + EPS)
    rows = row0 + lax.broadcasted_iota(jnp.int32, (x.shape[0], 1), 0)
    lat = rows < n_lat
    sc = jnp.where(lat, sc_b, sc_c)
    sh = jnp.where(lat, sh_b, sh_c)
    return xn * (1.0 + sc) + sh


def _mod_specs(D, base, B, col):
    return [pl.BlockSpec((1, 1, D), lambda *g: (base + g[0], 0, col)),
            pl.BlockSpec((1, 1, D), lambda *g: (base + B, 0, col))]


def _inproj_kernel(x_ref, shb, shc, scb, scc, w_ref, o_ref, h_ref, *, n_lat, tm):
    @pl.when(pl.program_id(2) == 0)
    def _():
        h = _norm_mod(x_ref[0], pl.program_id(1) * tm, n_lat, shb[0], scb[0], shc[0], scc[0])
        h_ref[...] = h.astype(h_ref.dtype)
    o_ref[0] = jnp.dot(h_ref[...], w_ref[...], preferred_element_type=F32).astype(o_ref.dtype)


def _inproj(xs, mod, w, base, n_lat):
    B, S, D = xs.shape
    NP = w.shape[1]
    tm = _tile(S, 1024)
    tn = _tile(NP, 1024)
    return pl.pallas_call(
        functools.partial(_inproj_kernel, n_lat=n_lat, tm=tm),
        out_shape=jax.ShapeDtypeStruct((B, S, NP), _ACT),
        grid=(B, S // tm, NP // tn),
        in_specs=[pl.BlockSpec((1, tm, D), lambda b, i, j: (b, i, 0))]
        + _mod_specs(D, base, B, 0) + _mod_specs(D, base, B, 1)
        + [pl.BlockSpec((D, tn), lambda b, i, j: (0, j))],
        out_specs=pl.BlockSpec((1, tm, tn), lambda b, i, j: (b, i, j)),
        scratch_shapes=[pltpu.VMEM((tm, D), _MXU)],
        compiler_params=_params("parallel", "parallel", "arbitrary"),
        name="in_proj",
    )(xs, mod, mod, mod, mod, w)


def _norm_rope(z, gct, gst, scale):
    lane = lax.broadcasted_iota(jnp.int32, (1, HEAD_PAD), 1)
    zz = jnp.where(lane < QK_HEAD, z * z, 0.0)
    r = lax.rsqrt(jnp.sum(zz, axis=-1, keepdims=True) * (1.0 / QK_HEAD) + EPS)
    zr = pltpu.roll(z, HEAD_PAD - QK_ROPE, axis=1)
    return (z * gct + zr * gst) * (r * scale)


def _mla_kernel(p_ref, wq_ref, wkv_ref, gcq_ref, gckv_ref, gq_ref, gk_ref, ct_ref, st_ref,
                q_ref, k_ref, v_ref):
    p = p_ref[0].astype(F32)
    pq = p[:, :Q_LORA]
    pkv = p[:, Q_LORA:Q_LORA + KV_LORA]
    xk = p[:, MLA_BLOCK - HEAD_PAD:]
    cq = pq * lax.rsqrt(jnp.mean(pq * pq, axis=-1, keepdims=True) + EPS) * gcq_ref[...]
    ckv = pkv * lax.rsqrt(jnp.mean(pkv * pkv, axis=-1, keepdims=True) + EPS) * gckv_ref[...]
    qf = _dot(cq, wq_ref[...])
    kvf = _dot(ckv, wkv_ref[...])
    ct = ct_ref[...]
    st = st_ref[...]
    gq = gq_ref[...]
    gk = gk_ref[...]
    q_gct, q_gst = gq[0:1] * ct, gq[1:2] * st
    k_gct, k_gst = gk[0:1] * ct, gk[1:2] * st
    lane = lax.broadcasted_iota(jnp.int32, (1, HEAD_PAD), 1)
    for h in range(MLA_HEADS):
        sl = slice(h * HEAD_PAD, (h + 1) * HEAD_PAD)
        q_ref[0, :, sl] = _norm_rope(qf[:, sl], q_gct, q_gst, ATTN_SCALE).astype(q_ref.dtype)
        kvh = kvf[:, sl]
        zk = jnp.where(lane < QK_NOPE, kvh, xk)
        k_ref[0, :, sl] = _norm_rope(zk, k_gct, k_gst, 1.0).astype(k_ref.dtype)
        v_ref[0, :, h * V_HEAD:(h + 1) * V_HEAD] = kvh[:, QK_NOPE:].astype(v_ref.dtype)


def _mla_front(P, wq, wkv, gcq, gckv, gq2, gk2, ct, st):
    B, S, _ = P.shape
    tm = _tile(S, 512)
    HW = MLA_HEADS * HEAD_PAD
    cblk = COL_MLA // MLA_BLOCK
    full = lambda shape: pl.BlockSpec(shape, lambda b, i: (0,) * len(shape))
    return pl.pallas_call(
        _mla_kernel,
        out_shape=(jax.ShapeDtypeStruct((B, S, HW), _ACT),
                   jax.ShapeDtypeStruct((B, S, HW), _ACT),
                   jax.ShapeDtypeStruct((B, S, MLA_WIDTH), _ACT)),
        grid=(B, S // tm),
        in_specs=[pl.BlockSpec((1, tm, MLA_BLOCK), lambda b, i: (b, i, cblk)),
                  full(wq.shape), full(wkv.shape), full(gcq.shape), full(gckv.shape),
                  full(gq2.shape), full(gk2.shape),
                  pl.BlockSpec((tm, HEAD_PAD), lambda b, i: (i, 0)),
                  pl.BlockSpec((tm, HEAD_PAD), lambda b, i: (i, 0))],
        out_specs=(pl.BlockSpec((1, tm, HW), lambda b, i: (b, i, 0)),
                   pl.BlockSpec((1, tm, HW), lambda b, i: (b, i, 0)),
                   pl.BlockSpec((1, tm, MLA_WIDTH), lambda b, i: (b, i, 0))),
        compiler_params=_params("parallel", "parallel"),
        name="mla_front",
    )(P, wq, wkv, gcq, gckv, gq2, gk2, ct, st)


def _attn_kernel(q_ref, k_ref, v_ref, o_ref, *, n_lat, tq):
    q = q_ref[0]

    def run(k, v):
        s = lax.dot_general(q, k, (((1,), (1,)), ((), ())), preferred_element_type=F32)
        m = jnp.max(s, axis=-1, keepdims=True)
        p = jnp.exp(s - m)
        l = jnp.sum(p, axis=-1, keepdims=True)
        o = jnp.dot(p.astype(v.dtype), v, preferred_element_type=F32)
        o_ref[0] = (o / l).astype(o_ref.dtype)

    is_lat = pl.program_id(2) * tq < n_lat

    @pl.when(is_lat)
    def _():
        run(k_ref[0], v_ref[0])

    @pl.when(jnp.logical_not(is_lat))
    def _():
        run(k_ref[0, n_lat:, :], v_ref[0, n_lat:, :])


def _attention(q, k, v, n_lat):
    B, S, _ = q.shape
    tq = _tile(math.gcd(n_lat, S - n_lat), 256)
    return pl.pallas_call(
        functools.partial(_attn_kernel, n_lat=n_lat, tq=tq),
        out_shape=jax.ShapeDtypeStruct((B, S, MLA_WIDTH), _ACT),
        grid=(B, MLA_HEADS, S // tq),
        in_specs=[pl.BlockSpec((1, tq, HEAD_PAD), lambda b, h, i: (b, i, h)),
                  pl.BlockSpec((1, S, HEAD_PAD), lambda b, h, i: (b, 0, h)),
                  pl.BlockSpec((1, S, V_HEAD), lambda b, h, i: (b, 0, h))],
        out_specs=pl.BlockSpec((1, tq, V_HEAD), lambda b, h, i: (b, i, h)),
        compiler_params=_params("parallel", "parallel", "arbitrary"),
        name="attention",
    )(q, k, v)


def _gelu_tanh(x):
    return 0.5 * x * (1.0 + jnp.tanh(math.sqrt(2.0 / math.pi) * (x + 0.044715 * (x * x * x))))


def _sgu_kernel(p_ref, g_ref, w_ref, b_ref, o_ref, *, tm):
    gp = _gelu_tanh(p_ref[0].astype(F32))
    u = gp[:, :SGU_WIDTH]
    v = gp[:, SGU_WIDTH:]
    v = v * lax.rsqrt(jnp.mean(v * v, axis=-1, keepdims=True) + EPS) * g_ref[...]
    vb = v.astype(_MXU)
    bias = b_ref[...]
    for c in range(tm // SGU_CHUNK):
        rs = slice(c * SGU_CHUNK, (c + 1) * SGU_CHUNK)
        for g in range(SGU_GROUPS):
            cs = slice(g * SGU_GROUP_CH, (g + 1) * SGU_GROUP_CH)
            mix = jnp.dot(w_ref[g], vb[rs, cs], preferred_element_type=F32) + bias[:, cs]
            o_ref[0, rs, cs] = (u[rs, cs] * mix).astype(o_ref.dtype)


def _sgu(P, g_sgu, w_sgu, bias):
    B, S, _ = P.shape
    tm = _tile(S, 512)
    full = lambda shape: pl.BlockSpec(shape, lambda b, i: (0,) * len(shape))
    return pl.pallas_call(
        functools.partial(_sgu_kernel, tm=tm),
        out_shape=jax.ShapeDtypeStruct((B, S, SGU_WIDTH), _ACT),
        grid=(B, S // tm),
        in_specs=[pl.BlockSpec((1, tm, 2 * SGU_WIDTH), lambda b, i: (b, i, COL_SGU // (2 * SGU_WIDTH))),
                  full(g_sgu.shape), full(w_sgu.shape), full(bias.shape)],
        out_specs=pl.BlockSpec((1, tm, SGU_WIDTH), lambda b, i: (b, i, 0)),
        compiler_params=_params("parallel", "parallel"),
        name="spatial_gating",
    )(P, g_sgu, w_sgu, bias)


def _window_sum(x, g):
    n, C = x.shape
    z = jnp.zeros((POOL_HALO, C), F32)
    xe = jnp.concatenate([z, x, z], axis=0)
    ne = n + 2 * POOL_HALO
    w = xe + pltpu.roll(xe, 1, axis=0)
    for s in range(g):
        k = 1 << s
        w = pltpu.roll(w, k, axis=0) + pltpu.roll(w, ne - k, axis=0)
    return w[POOL_HALO:POOL_HALO + n]


def _pool_kernel(p_ref, w_ref, s_ref, o_ref, *, n_lat):
    S = p_ref.shape[1]
    for lo, n in ((0, n_lat), (n_lat, S - n_lat)):
        t = lax.broadcasted_iota(jnp.int32, (n, 1), 0)
        for g, win in enumerate(POOL_WINDOWS):
            cs = slice(g * POOL_GROUP_CH, (g + 1) * POOL_GROUP_CH)
            x = p_ref[0, lo:lo + n, cs].astype(F32)
            half = win // 2
            cnt = (jnp.minimum(t + half, n) - jnp.maximum(t - half, 0)).astype(F32)
            m = _window_sum(x, g) / cnt - x
            y = _dot(m, w_ref[g]) * s_ref[:, cs]
            o_ref[0, lo:lo + n, cs] = y.astype(o_ref.dtype)


def _pool(P, w_pool, s_pool, n_lat):
    B, S, _ = P.shape
    full = lambda shape: pl.BlockSpec(shape, lambda b: (0,) * len(shape))
    return pl.pallas_call(
        functools.partial(_pool_kernel, n_lat=n_lat),
        out_shape=jax.ShapeDtypeStruct((B, S, POOL_WIDTH), _ACT),
        grid=(B,),
        in_specs=[pl.BlockSpec((1, S, POOL_WIDTH), lambda b: (b, 0, COL_POOL // POOL_WIDTH)),
                  full(w_pool.shape), full(s_pool.shape)],
        out_specs=pl.BlockSpec((1, S, POOL_WIDTH), lambda b: (b, 0, 0)),
        compiler_params=_params("parallel"),
        name="multiscale_pool",
    )(P, w_pool, s_pool)


def _merge_kernel(a_ref, b_ref, c_ref, ga_ref, gb_ref, gc_ref, x_ref, g1b, g1c,
                  wa_ref, wb_ref, wc_ref, wo_ref, o_ref, *, n_lat, tm):
    def gate(ref):
        return jax.nn.sigmoid(ref[0].astype(F32))
    y = gate(ga_ref) * jnp.dot(a_ref[0], wa_ref[...], preferred_element_type=F32)
    y = y + gate(gb_ref) * jnp.dot(b_ref[0], wb_ref[...], preferred_element_type=F32)
    y = y + gate(gc_ref) * jnp.dot(c_ref[0], wc_ref[...], preferred_element_type=F32)
    out = _dot(y, wo_ref[...])
    rows = pl.program_id(1) * tm + lax.broadcasted_iota(jnp.int32, (tm, 1), 0)
    g1 = jnp.where(rows < n_lat, g1b[0], g1c[0])
    o_ref[0] = x_ref[0] + g1 * out


def _merge(a, b, cp, P, xs, mod, wa, wb, wc, wo, base, n_lat):
    B, S, D = xs.shape
    tm = _tile(S, 256)
    gblk = COL_GATE // D
    tok = lambda w: pl.BlockSpec((1, tm, w), lambda bb, i: (bb, i, 0))
    gate = lambda k: pl.BlockSpec((1, tm, D), lambda bb, i: (bb, i, gblk + k))
    wres = lambda shape: pl.BlockSpec(shape, lambda bb, i: (0, 0), pipeline_mode=pl.Buffered(1))
    return pl.pallas_call(
        functools.partial(_merge_kernel, n_lat=n_lat, tm=tm),
        out_shape=jax.ShapeDtypeStruct((B, S, D), F32),
        grid=(B, S // tm),
        in_specs=[tok(MLA_WIDTH), tok(SGU_WIDTH), tok(POOL_WIDTH), gate(0), gate(1), gate(2), tok(D)]
        + _mod_specs(D, base, B, 2)
        + [wres(wa.shape), wres(wb.shape), wres(wc.shape), wres(wo.shape)],
        out_specs=tok(D),
        compiler_params=_params("parallel", "parallel"),
        name="merge_out",
    )(a, b, cp, P, P, P, xs, mod, mod, wa, wb, wc, wo)


def _router_kernel(x_ref, shb, shc, scb, scc, w_ref, b_ref, h_ref, r_ref, *, n_lat, tm):
    h = _norm_mod(x_ref[0], pl.program_id(1) * tm, n_lat, shb[0], scb[0], shc[0], scc[0])
    h_ref[0] = h.astype(h_ref.dtype)
    logits = jnp.dot(h, w_ref[...], preferred_element_type=F32,
                     precision=lax.Precision.HIGHEST) + b_ref[...]
    lane = lax.broadcasted_iota(jnp.int32, (tm, ROUTE_COLS), 1)
    ninf = -jnp.inf
    lg = jnp.where(lane < N_GROUPS, logits, ninf)
    mg = jnp.max(lg, axis=-1, keepdims=True)
    g_w = 1.0 / jnp.sum(jnp.exp(lg - mg), axis=-1, keepdims=True)
    g_i = jnp.min(jnp.where(lg == mg, lane, ROUTE_COLS), axis=-1, keepdims=True)
    lo = N_GROUPS + g_i * EXPERTS_PER_GROUP
    le = jnp.where((lane >= lo) & (lane < lo + EXPERTS_PER_GROUP), logits, ninf)
    m1 = jnp.max(le, axis=-1, keepdims=True)
    i1 = jnp.min(jnp.where(le == m1, lane, ROUTE_COLS), axis=-1, keepdims=True)
    le2 = jnp.where(lane == i1, ninf, le)
    m2 = jnp.max(le2, axis=-1, keepdims=True)
    i2 = jnp.min(jnp.where(le2 == m2, lane, ROUTE_COLS), axis=-1, keepdims=True)
    e21 = jnp.exp(m2 - m1)
    w1 = 1.0 / (1.0 + e21)
    w2 = e21 * w1
    out = jnp.where(lane == 0, (i1 - N_GROUPS).astype(F32), 0.0)
    out = jnp.where(lane == 1, (i2 - N_GROUPS).astype(F32), out)
    out = jnp.where(lane == 2, g_w * w1, out)
    out = jnp.where(lane == 3, g_w * w2, out)
    r_ref[0] = out


def _router(xs, mod, w_r, b_r, base, n_lat):
    B, S, D = xs.shape
    tm = _tile(S, 512)
    full = lambda shape: pl.BlockSpec(shape, lambda b, i: (0,) * len(shape))
    return pl.pallas_call(
        functools.partial(_router_kernel, n_lat=n_lat, tm=tm),
        out_shape=(jax.ShapeDtypeStruct((B, S, D), _ACT),
                   jax.ShapeDtypeStruct((B, S, ROUTE_COLS), F32)),
        grid=(B, S // tm),
        in_specs=[pl.BlockSpec((1, tm, D), lambda b, i: (b, i, 0))]
        + _mod_specs(D, base, B, 3) + _mod_specs(D, base, B, 4)
        + [full(w_r.shape), full(b_r.shape)],
        out_specs=(pl.BlockSpec((1, tm, D), lambda b, i: (b, i, 0)),
                   pl.BlockSpec((1, tm, ROUTE_COLS), lambda b, i: (b, i, 0))),
        compiler_params=_params("parallel", "parallel"),
        name="moe_router",
    )(xs, mod, mod, mod, mod, w_r, b_r)


def _dispatch_kernel(pos_ref, zlo_ref, h_ref, o_ref, zero_ref, sem, *, td):
    step = pl.program_id(0)

    def zero_fill(e):
        return pltpu.make_async_copy(zero_ref, o_ref.at[pl.ds(zlo_ref[e], MOE_TILE)], sem)

    @pl.when(step == 0)
    def _():
        zero_ref[...] = jnp.zeros_like(zero_ref)
        for e in range(2 * N_EXPERTS):
            @pl.when(zlo_ref[e] >= 0)
            def _():
                zero_fill(e).start()
        for e in range(2 * N_EXPERTS):
            @pl.when(zlo_ref[e] >= 0)
            def _():
                zero_fill(e).wait()

    base = step * td

    def row_copy(r, k):
        return pltpu.make_async_copy(h_ref.at[r], o_ref.at[pos_ref[2 * (base + r) + k]], sem)

    def issue(r, carry):
        row_copy(r, 0).start()
        row_copy(r, 1).start()
        return carry

    def drain(r, carry):
        row_copy(r, 0).wait()
        row_copy(r, 1).wait()
        return carry

    lax.fori_loop(0, td, issue, 0)
    lax.fori_loop(0, td, drain, 0)


def _dispatch(pos, zlo, h3, n_rows):
    T, Q, _ = h3.shape
    td = _tile(T, 512, SUBLANES)
    return pl.pallas_call(
        functools.partial(_dispatch_kernel, td=td),
        out_shape=jax.ShapeDtypeStruct((n_rows, Q, LANES), h3.dtype),
        grid_spec=pltpu.PrefetchScalarGridSpec(
            num_scalar_prefetch=2,
            grid=(T // td,),
            in_specs=[pl.BlockSpec((td, Q, LANES), lambda i, pos, zlo: (i, 0, 0))],
            out_specs=pl.BlockSpec(memory_space=pl.ANY),
            scratch_shapes=[pltpu.VMEM((MOE_TILE, Q, LANES), h3.dtype), pltpu.SemaphoreType.DMA(())],
        ),
        compiler_params=pltpu.CompilerParams(dimension_semantics=("arbitrary",),
                                             vmem_limit_bytes=VMEM_LIMIT, has_side_effects=True),
        name="moe_dispatch",
    )(pos, zlo, h3)


def _expert_kernel(te_ref, nv_ref, x_ref, wgu_ref, wd_ref, o_ref):
    used = pl.program_id(0) < nv_ref[0]

    @pl.when(used)
    def _():
        hgu = jnp.dot(x_ref[...], wgu_ref[0], preferred_element_type=F32)
        hg = hgu[:, :EXPERT_HIDDEN]
        hid = hg * jax.nn.sigmoid(hg) * hgu[:, EXPERT_HIDDEN:]
        o_ref[...] = _dot(hid, wd_ref[0]).astype(o_ref.dtype)

    @pl.when(jnp.logical_not(used))
    def _():
        o_ref[...] = jnp.zeros_like(o_ref)


def _experts(tile_e, nvalid, xsorted, wgu, wd):
    NR, D = xsorted.shape
    NT = NR // MOE_TILE
    return pl.pallas_call(
        _expert_kernel,
        out_shape=jax.ShapeDtypeStruct((NR, D), _ACT),
        grid_spec=pltpu.PrefetchScalarGridSpec(
            num_scalar_prefetch=2,
            grid=(NT,),
            in_specs=[pl.BlockSpec((MOE_TILE, D), lambda j, te, nv: (j, 0)),
                      pl.BlockSpec((1, D, 2 * EXPERT_HIDDEN), lambda j, te, nv: (te[j], 0, 0)),
                      pl.BlockSpec((1, EXPERT_HIDDEN, D), lambda j, te, nv: (te[j], 0, 0))],
            out_specs=pl.BlockSpec((MOE_TILE, D), lambda j, te, nv: (j, 0)),
        ),
        compiler_params=_params("arbitrary"),
        name="moe_experts",
    )(tile_e, nvalid, xsorted, wgu, wd)


def _combine_kernel(pos_ref, y_ref, x_ref, cw_ref, g2b, g2c, o_ref, buf, sem, *, n_lat, tc, S):
    b = pl.program_id(0)
    i = pl.program_id(1)
    base = b * S + i * tc

    def row_copy(r, k):
        return pltpu.make_async_copy(y_ref.at[pos_ref[2 * (base + r) + k]], buf.at[k, r], sem)

    def issue(r, carry):
        row_copy(r, 0).start()
        row_copy(r, 1).start()
        return carry

    def drain(r, carry):
        row_copy(r, 0).wait()
        row_copy(r, 1).wait()
        return carry

    lax.fori_loop(0, tc, issue, 0)
    lax.fori_loop(0, tc, drain, 0)
    cw = cw_ref[0]
    y = buf[0].astype(F32) * cw[:, 0:1, :] + buf[1].astype(F32) * cw[:, 1:2, :]
    rows = i * tc + lax.broadcasted_iota(jnp.int32, (tc, 1, 1), 0)
    g2 = jnp.where(rows < n_lat, g2b[...], g2c[...])
    o_ref[0] = x_ref[0] + g2 * y


def _combine(pos, ysorted3, xs4, cw, mod3, base, n_lat):
    B, S, Q, _ = xs4.shape
    tc = _tile(S, 256, SUBLANES)
    return pl.pallas_call(
        functools.partial(_combine_kernel, n_lat=n_lat, tc=tc, S=S),
        out_shape=jax.ShapeDtypeStruct(xs4.shape, F32),
        grid_spec=pltpu.PrefetchScalarGridSpec(
            num_scalar_prefetch=1,
            grid=(B, S // tc),
            in_specs=[pl.BlockSpec(memory_space=pl.ANY),
                      pl.BlockSpec((1, tc, Q, LANES), lambda b, i, pos: (b, i, 0, 0)),
                      pl.BlockSpec((1, tc, 2, LANES), lambda b, i, pos: (b, i, 0, 0)),
                      pl.BlockSpec((1, Q, LANES), lambda b, i, pos: (base + b, 5, 0)),
                      pl.BlockSpec((1, Q, LANES), lambda b, i, pos: (base + B, 5, 0))],
            out_specs=pl.BlockSpec((1, tc, Q, LANES), lambda b, i, pos: (b, i, 0, 0)),
            scratch_shapes=[pltpu.VMEM((2, tc, Q, LANES), ysorted3.dtype), pltpu.SemaphoreType.DMA(())],
        ),
        compiler_params=_params("arbitrary", "arbitrary"),
        name="moe_combine",
    )(pos, ysorted3, xs4, cw, mod3, mod3)


def _swap_perm():
    j = np.arange(QK_ROPE)
    a, h, f = j // ROPE_AXIS_DIM, (j // ROPE_FREQS) % 2, j % ROPE_FREQS
    return a * ROPE_AXIS_DIM + (1 - h) * ROPE_FREQS + f


def _rope_tables(n_lat, S):
    t = jnp.arange(n_lat)
    row = (t // GRID_W).astype(F32)
    col = (t % GRID_W).astype(F32)
    inv = ROPE_THETA ** (-jnp.arange(0, ROPE_AXIS_DIM, 2, dtype=F32) / ROPE_AXIS_DIM)
    a0, a1 = row[:, None] * inv, col[:, None] * inv
    c64 = jnp.concatenate([jnp.cos(a0), jnp.cos(a0), jnp.cos(a1), jnp.cos(a1)], axis=1)
    s64 = jnp.concatenate([-jnp.sin(a0), jnp.sin(a0), -jnp.sin(a1), jnp.sin(a1)], axis=1)
    nc = S - n_lat
    c64 = jnp.concatenate([c64, jnp.ones((nc, QK_ROPE), F32)], axis=0)
    s64 = jnp.concatenate([s64, jnp.zeros((nc, QK_ROPE), F32)], axis=0)
    ct = jnp.concatenate([jnp.ones((S, QK_NOPE), F32), c64, jnp.zeros((S, QK_ROPE), F32)], axis=1)
    st = jnp.concatenate([jnp.zeros((S, QK_NOPE), F32), s64, jnp.zeros((S, QK_ROPE), F32)], axis=1)
    return ct, st


def _head_gains(g, perm):
    z = jnp.zeros((QK_ROPE,), F32)
    ga = jnp.concatenate([g[:QK_NOPE], g[QK_NOPE:], z])
    gb = jnp.concatenate([jnp.zeros((QK_NOPE,), F32), g[QK_NOPE:][perm], z])
    return jnp.stack([ga, gb], axis=0)


def _prep_w_in(w, D, perm):
    o = np.cumsum((0, Q_LORA, KV_LORA, QK_ROPE, 2 * SGU_WIDTH, POOL_WIDTH, 3 * D))
    wq, wkv, wkr, wsgu, wpool, wgate = (w[:, o[k]:o[k + 1]] for k in range(6))
    pad = jnp.zeros((D, MLA_BLOCK - Q_LORA - KV_LORA - 2 * QK_ROPE), w.dtype)
    return jnp.concatenate([wsgu, wpool, wq, wkv, pad, wkr, wkr[:, perm], wgate], axis=1).astype(_MXU)


def _slots(route, T):
    e = route[:, :2].astype(jnp.int32)
    oh = (e[:, :, None] == jnp.arange(N_EXPERTS, dtype=jnp.int32)).astype(jnp.int32).sum(axis=1)
    csum = jnp.cumsum(oh, axis=0)
    rank = csum - oh
    cnt = csum[-1]
    padded = (cnt + MOE_TILE - 1) // MOE_TILE * MOE_TILE
    end = jnp.cumsum(padded)
    off = end - padded
    pos = off[e] + jnp.take_along_axis(rank, e, axis=1)
    n_rows = 2 * T + N_EXPERTS * MOE_TILE
    NT = n_rows // MOE_TILE
    nvalid = (end[-1] // MOE_TILE).astype(jnp.int32)
    tiles = jnp.minimum(jnp.arange(NT, dtype=jnp.int32), nvalid - 1)
    tile_e = jnp.searchsorted(end, tiles * MOE_TILE, side="right").astype(jnp.int32)
    tail = nvalid + jnp.arange(N_EXPERTS, dtype=jnp.int32)
    zlo = jnp.concatenate([jnp.where(cnt > 0, end - MOE_TILE, -1),
                           jnp.where(tail < NT, tail * MOE_TILE, -1)]).astype(jnp.int32)
    return pos.reshape(-1).astype(jnp.int32), zlo, tile_e, nvalid.reshape(1), n_rows


def kernel(x, c, ctx, c_ctx, w_ada, b_ada, w_in, g_cq, g_ckv, w_uq, w_ukv, g_q, g_k, w_sgu, b_sgu, g_sgu,
           w_pool, s_pool, w_ao, w_bo, w_co, w_out, w_rg, b_rg, w_re, b_re, w_e_gate, w_e_up, w_e_down):
    B, n, D = x.shape
    nc = ctx.shape[1]
    S = n + nc
    T = B * S
    L = w_ada.shape[0]
    Q = D // LANES
    perm = _swap_perm()

    R = -(-(B + 1) // SUBLANES) * SUBLANES
    cc = jnp.concatenate([c, c_ctx[None], jnp.zeros((R - B - 1, D), F32)], axis=0)
    mod = _ada(cc, w_ada, b_ada)
    mod2 = mod.reshape(L * R, 1, 6 * D)
    mod3 = mod.reshape(L * R, 6 * Q, LANES)

    ct, st = _rope_tables(n, S)
    xs = jnp.concatenate([x, ctx], axis=1)

    for i in range(L):
        base = i * R
        w_in_r = _prep_w_in(w_in[i], D, perm)
        wq = w_uq[i].reshape(Q_LORA, MLA_HEADS, QK_HEAD)
        wq = jnp.concatenate([wq, wq[:, :, QK_NOPE:][:, :, perm]], axis=2).reshape(Q_LORA, -1).astype(_MXU)
        wkv = w_ukv[i].astype(_MXU)
        bias = jnp.repeat(b_sgu[i].T, SGU_GROUP_CH, axis=1)
        w_r = jnp.concatenate([w_rg[i], jnp.moveaxis(w_re[i], 0, 1).reshape(D, N_EXPERTS),
                               jnp.zeros((D, ROUTE_COLS - N_GROUPS - N_EXPERTS), F32)], axis=1)
        b_r = jnp.concatenate([b_rg[i], b_re[i].reshape(-1),
                               jnp.zeros((ROUTE_COLS - N_GROUPS - N_EXPERTS,), F32)])[None]
        wgu = jnp.concatenate([w_e_gate[i], w_e_up[i]], axis=2).astype(_MXU)
        wd = w_e_down[i].astype(_MXU)

        P = _inproj(xs, mod2, w_in_r, base, n)
        q, k, v = _mla_front(P, wq, wkv, g_cq[i][None], g_ckv[i][None],
                             _head_gains(g_q[i], perm), _head_gains(g_k[i], perm), ct, st)
        a = _attention(q, k, v, n)
        bb = _sgu(P, g_sgu[i][None], w_sgu[i].astype(_MXU), bias)
        cp = _pool(P, w_pool[i].astype(_MXU), s_pool[i][None], n)
        xs = _merge(a, bb, cp, P, xs, mod2, w_ao[i].astype(_MXU), w_bo[i].astype(_MXU),
                    w_co[i].astype(_MXU), w_out[i].astype(_MXU), base, n)

        h2, route = _router(xs, mod2, w_r, b_r, base, n)
        route = route.reshape(T, ROUTE_COLS)
        pos, zlo, tile_e, nvalid, n_rows = _slots(route, T)
        xsorted = _dispatch(pos, zlo, h2.reshape(T, Q, LANES), n_rows)
        ysorted = _experts(tile_e, nvalid, xsorted.reshape(n_rows, D), wgu, wd)
        cw = jnp.broadcast_to(route[:, 2:4, None], (T, 2, LANES)).reshape(B, S, 2, LANES)
        xs = _combine(pos, ysorted.reshape(n_rows, Q, LANES), xs.reshape(B, S, Q, LANES), cw,
                      mod3, base, n).reshape(B, S, D)

    return xs[:, :n, :]
```

```python
import functools
import math

import jax
import jax.numpy as jnp
import numpy as np
from jax import lax
from jax.experimental import pallas as pl
from jax.experimental.pallas import tpu as pltpu

GRID_W = 64
EPS = 1e-6
MLA_HEADS = 8
QK_NOPE = 128
QK_ROPE = 64
V_HEAD = 128
QK_HEAD = QK_NOPE + QK_ROPE
Q_LORA = 512
KV_LORA = 256
MLA_WIDTH = MLA_HEADS * V_HEAD
ROPE_THETA = 10000.0
ROPE_AXIS_DIM = QK_ROPE // 2
ROPE_FREQS = ROPE_AXIS_DIM // 2
ATTN_SCALE = 1.0 / math.sqrt(QK_HEAD)
LOG2E = math.log2(math.e)
SGU_GROUPS = 8
SGU_CHUNK = 128
SGU_GROUP_CH = 128
SGU_WIDTH = SGU_GROUPS * SGU_GROUP_CH
POOL_WINDOWS = (2, 4, 8, 16)
POOL_GROUP_CH = 256
POOL_WIDTH = len(POOL_WINDOWS) * POOL_GROUP_CH
N_GROUPS = 4
EXPERTS_PER_GROUP = 8
N_EXPERTS = N_GROUPS * EXPERTS_PER_GROUP
EXPERT_HIDDEN = 512

LANES = 128
SUBLANES = 8
HEAD_PAD = 256
VMEM_LIMIT = 56 * 1024 * 1024

MLA_BLOCK = 1024
COL_SGU = 0
COL_POOL = 2 * SGU_WIDTH
COL_MLA = COL_POOL + POOL_WIDTH
COL_GATE = COL_MLA + MLA_BLOCK

POOL_HALO = 16
ROUTE_COLS = LANES
MOE_TILE = 256

_MXU = jnp.bfloat16
_ACT = jnp.bfloat16
F32 = jnp.float32
U32 = jnp.uint32


def _params(*sem):
    return pltpu.CompilerParams(dimension_semantics=sem, vmem_limit_bytes=VMEM_LIMIT)


def _tile(total, cap, mult=LANES):
    best = None
    for t in range(mult, min(total, cap) + 1, mult):
        if total % t == 0:
            best = t
    assert best is not None, (total, cap, mult)
    return best


def _dot(a, b):
    return jnp.dot(a.astype(_MXU), b.astype(_MXU), preferred_element_type=F32)


def _ada_kernel(c_ref, w_ref, b_ref, o_ref):
    c = c_ref[...]
    s = c * jax.nn.sigmoid(c)
    o_ref[0] = _dot(s, w_ref[0]) + b_ref[0]


def _ada(cc, w_ada, b_ada):
    L, D, N = w_ada.shape
    R = cc.shape[0]
    tn = _tile(N, 1024)
    return pl.pallas_call(
        _ada_kernel,
        out_shape=jax.ShapeDtypeStruct((L, R, N), F32),
        grid=(L, N // tn),
        in_specs=[
            pl.BlockSpec((R, D), lambda l, j: (0, 0)),
            pl.BlockSpec((1, D, tn), lambda l, j: (l, 0, j)),
            pl.BlockSpec((1, 1, tn), lambda l, j: (l, 0, j)),
        ],
        out_specs=pl.BlockSpec((1, R, tn), lambda l, j: (l, 0, j)),
        compiler_params=_params("parallel", "parallel"),
        name="ada_mod",
    )(cc, w_ada, b_ada.reshape(L, 1, N))


def _norm_mod(x, row0, n_lat, sh_b, sc_b, sh_c, sc_c):
    ms = jnp.mean(x * x, axis=-1, keepdims=True)
    xn = x * lax.rsqrt(ms + EPS)
    rows = row0 + lax.broadcasted_iota(jnp.int32, (x.shape[0], 1), 0)
    lat = rows < n_lat
    sc = jnp.where(lat, sc_b, sc_c)
    sh = jnp.where(lat, sh_b, sh_c)
    return xn * (1.0 + sc) + sh


def _mod_specs(D, base, B, col):
    return [pl.BlockSpec((1, 1, D), lambda *g: (base + g[0], 0, col)),
            pl.BlockSpec((1, 1, D), lambda *g: (base + B, 0, col))]


def _inproj_kernel(x_ref, shb, shc, scb, scc, w_ref, o_ref, h_ref, *, n_lat, tm):
    @pl.when(pl.program_id(2) == 0)
    def _():
        h = _norm_mod(x_ref[0], pl.program_id(1) * tm, n_lat, shb[0], scb[0], shc[0], scc[0])
        h_ref[...] = h.astype(h_ref.dtype)
    o_ref[0] = jnp.dot(h_ref[...], w_ref[...], preferred_element_type=F32).astype(o_ref.dtype)


def _inproj(xs, mod, w, base, n_lat):
    B, S, D = xs.shape
    NP = w.shape[1]
    tm = _tile(S, 1024)
    tn = _tile(NP, 1024)
    return pl.pallas_call(
        functools.partial(_inproj_kernel, n_lat=n_lat, tm=tm),
        out_shape=jax.ShapeDtypeStruct((B, S, NP), _ACT),
        grid=(B, S // tm, NP // tn),
        in_specs=[pl.BlockSpec((1, tm, D), lambda b, i, j: (b, i, 0))]
        + _mod_specs(D, base, B, 0) + _mod_specs(D, base, B, 1)
        + [pl.BlockSpec((D, tn), lambda b, i, j: (0, j))],
        out_specs=pl.BlockSpec((1, tm, tn), lambda b, i, j: (b, i, j)),
        scratch_shapes=[pltpu.VMEM((tm, D), _MXU)],
        compiler_params=_params("parallel", "parallel", "arbitrary"),
        name="in_proj",
    )(xs, mod, mod, mod, mod, w)


def _norm_rope(z, gct, gst, scale):
    lane = lax.broadcasted_iota(jnp.int32, (1, HEAD_PAD), 1)
    zz = jnp.where(lane < QK_HEAD, z * z, 0.0)
    r = lax.rsqrt(jnp.sum(zz, axis=-1, keepdims=True) * (1.0 / QK_HEAD) + EPS)
    zr = pltpu.roll(z, HEAD_PAD - QK_ROPE, axis=1)
    return (z * gct + zr * gst) * (r * scale)


def _mla_kernel(p_ref, wq_ref, wkv_ref, gcq_ref, gckv_ref, gq_ref, gk_ref, ct_ref, st_ref,
                q_ref, k_ref, v_ref):
    p = p_ref[0].astype(F32)
    pq = p[:, :Q_LORA]
    pkv = p[:, Q_LORA:Q_LORA + KV_LORA]
    xk = p[:, MLA_BLOCK - HEAD_PAD:]
    cq = pq * lax.rsqrt(jnp.mean(pq * pq, axis=-1, keepdims=True) + EPS) * gcq_ref[...]
    ckv = pkv * lax.rsqrt(jnp.mean(pkv * pkv, axis=-1, keepdims=True) + EPS) * gckv_ref[...]
    qf = _dot(cq, wq_ref[...])
    kvf = _dot(ckv, wkv_ref[...])
    ct = ct_ref[...]
    st = st_ref[...]
    gq = gq_ref[...]
    gk = gk_ref[...]
    q_gct, q_gst = gq[0:1] * ct, gq[1:2] * st
    k_gct, k_gst = gk[0:1] * ct, gk[1:2] * st
    lane = lax.broadcasted_iota(jnp.int32, (1, HEAD_PAD), 1)
    for h in range(MLA_HEADS):
        sl = slice(h * HEAD_PAD, (h + 1) * HEAD_PAD)
        q_ref[0, :, sl] = _norm_rope(qf[:, sl], q_gct, q_gst, ATTN_SCALE * LOG2E).astype(q_ref.dtype)
        kvh = kvf[:, sl]
        zk = jnp.where(lane < QK_NOPE, kvh, xk)
        k_ref[0, :, sl] = _norm_rope(zk, k_gct, k_gst, 1.0).astype(k_ref.dtype)
        v_ref[0, :, h * V_HEAD:(h + 1) * V_HEAD] = kvh[:, QK_NOPE:].astype(v_ref.dtype)


def _mla_front(P, wq, wkv, gcq, gckv, gq2, gk2, ct, st):
    B, S, _ = P.shape
    tm = _tile(S, 512)
    HW = MLA_HEADS * HEAD_PAD
    cblk = COL_MLA // MLA_BLOCK
    full = lambda shape: pl.BlockSpec(shape, lambda b, i: (0,) * len(shape))
    return pl.pallas_call(
        _mla_kernel,
        out_shape=(jax.ShapeDtypeStruct((B, S, HW), _ACT),
                   jax.ShapeDtypeStruct((B, S, HW), _ACT),
                   jax.ShapeDtypeStruct((B, S, MLA_WIDTH), _ACT)),
        grid=(B, S // tm),
        in_specs=[pl.BlockSpec((1, tm, MLA_BLOCK), lambda b, i: (b, i, cblk)),
                  full(wq.shape), full(wkv.shape), full(gcq.shape), full(gckv.shape),
                  full(gq2.shape), full(gk2.shape),
                  pl.BlockSpec((tm, HEAD_PAD), lambda b, i: (i, 0)),
                  pl.BlockSpec((tm, HEAD_PAD), lambda b, i: (i, 0))],
        out_specs=(pl.BlockSpec((1, tm, HW), lambda b, i: (b, i, 0)),
                   pl.BlockSpec((1, tm, HW), lambda b, i: (b, i, 0)),
                   pl.BlockSpec((1, tm, MLA_WIDTH), lambda b, i: (b, i, 0))),
        compiler_params=_params("parallel", "parallel"),
        name="mla_front",
    )(P, wq, wkv, gcq, gckv, gq2, gk2, ct, st)


def _attn_kernel(q_ref, k_ref, v_ref, o_ref, *, n_lat, tq, heads):
    def run(lo):
        nk = k_ref.shape[1] - lo
        ones = (lax.broadcasted_iota(jnp.int32, (nk, V_HEAD), 1) == 0).astype(v_ref.dtype)
        for h in range(heads):
            q = q_ref[0, :, h * HEAD_PAD:(h + 1) * HEAD_PAD]
            k = k_ref[0, lo:, h * HEAD_PAD:(h + 1) * HEAD_PAD]
            v = jnp.concatenate([v_ref[0, lo:, h * V_HEAD:(h + 1) * V_HEAD], ones], axis=1)
            s = lax.dot_general(q, k, (((1,), (1,)), ((), ())), preferred_element_type=F32)
            p = jnp.exp2(s - jnp.max(s, axis=-1, keepdims=True))
            o = jnp.dot(p.astype(v.dtype), v, preferred_element_type=F32)
            o_ref[0, :, h * V_HEAD:(h + 1) * V_HEAD] = (
                o[:, :V_HEAD] / o[:, V_HEAD:V_HEAD + 1]).astype(o_ref.dtype)

    is_lat = pl.program_id(2) * tq < n_lat

    @pl.when(is_lat)
    def _():
        run(0)

    @pl.when(jnp.logical_not(is_lat))
    def _():
        run(n_lat)


ATTN_HEADS_PER_STEP = 4


def _attention(q, k, v, n_lat):
    B, S, _ = q.shape
    tq = _tile(math.gcd(n_lat, S - n_lat), 256)
    G = ATTN_HEADS_PER_STEP
    return pl.pallas_call(
        functools.partial(_attn_kernel, n_lat=n_lat, tq=tq, heads=G),
        out_shape=jax.ShapeDtypeStruct((B, S, MLA_WIDTH), _ACT),
        grid=(B, MLA_HEADS // G, S // tq),
        in_specs=[pl.BlockSpec((1, tq, G * HEAD_PAD), lambda b, h, i: (b, i, h)),
                  pl.BlockSpec((1, S, G * HEAD_PAD), lambda b, h, i: (b, 0, h)),
                  pl.BlockSpec((1, S, G * V_HEAD), lambda b, h, i: (b, 0, h))],
        out_specs=pl.BlockSpec((1, tq, G * V_HEAD), lambda b, h, i: (b, i, h)),
        compiler_params=_params("parallel", "parallel", "arbitrary"),
        name="attention",
    )(q, k, v)


def _gelu_tanh(x):
    return 0.5 * x * (1.0 + jnp.tanh(math.sqrt(2.0 / math.pi) * (x + 0.044715 * (x * x * x))))


def _sgu_kernel(p_ref, g_ref, w_ref, b_ref, o_ref, *, tm):
    gp = _gelu_tanh(p_ref[0].astype(F32))
    u = gp[:, :SGU_WIDTH]
    v = gp[:, SGU_WIDTH:]
    v = v * lax.rsqrt(jnp.mean(v * v, axis=-1, keepdims=True) + EPS) * g_ref[...]
    vb = v.astype(_MXU)
    bias = b_ref[...]
    for c in range(tm // SGU_CHUNK):
        rs = slice(c * SGU_CHUNK, (c + 1) * SGU_CHUNK)
        for g in range(SGU_GROUPS):
            cs = slice(g * SGU_GROUP_CH, (g + 1) * SGU_GROUP_CH)
            mix = jnp.dot(w_ref[g], vb[rs, cs], preferred_element_type=F32) + bias[:, cs]
            o_ref[0, rs, cs] = (u[rs, cs] * mix).astype(o_ref.dtype)


def _sgu(P, g_sgu, w_sgu, bias):
    B, S, _ = P.shape
    tm = _tile(S, 512)
    full = lambda shape: pl.BlockSpec(shape, lambda b, i: (0,) * len(shape))
    return pl.pallas_call(
        functools.partial(_sgu_kernel, tm=tm),
        out_shape=jax.ShapeDtypeStruct((B, S, SGU_WIDTH), _ACT),
        grid=(B, S // tm),
        in_specs=[pl.BlockSpec((1, tm, 2 * SGU_WIDTH), lambda b, i: (b, i, COL_SGU // (2 * SGU_WIDTH))),
                  full(g_sgu.shape), full(w_sgu.shape), full(bias.shape)],
        out_specs=pl.BlockSpec((1, tm, SGU_WIDTH), lambda b, i: (b, i, 0)),
        compiler_params=_params("parallel", "parallel"),
        name="spatial_gating",
    )(P, g_sgu, w_sgu, bias)


def _window_sum(x, g):
    n, C = x.shape
    z = jnp.zeros((POOL_HALO, C), F32)
    xe = jnp.concatenate([z, x, z], axis=0)
    ne = n + 2 * POOL_HALO
    w = xe + pltpu.roll(xe, 1, axis=0)
    for s in range(g):
        k = 1 << s
        w = pltpu.roll(w, k, axis=0) + pltpu.roll(w, ne - k, axis=0)
    return w[POOL_HALO:POOL_HALO + n]


def _pool_kernel(p_ref, w_ref, s_ref, o_ref, *, n_lat):
    S = p_ref.shape[1]
    for lo, n in ((0, n_lat), (n_lat, S - n_lat)):
        t = lax.broadcasted_iota(jnp.int32, (n, 1), 0)
        for g, win in enumerate(POOL_WINDOWS):
            cs = slice(g * POOL_GROUP_CH, (g + 1) * POOL_GROUP_CH)
            x = p_ref[0, lo:lo + n, cs].astype(F32)
            half = win // 2
            cnt = (jnp.minimum(t + half, n) - jnp.maximum(t - half, 0)).astype(F32)
            m = _window_sum(x, g) / cnt - x
            y = _dot(m, w_ref[g]) * s_ref[:, cs]
            o_ref[0, lo:lo + n, cs] = y.astype(o_ref.dtype)


def _pool(P, w_pool, s_pool, n_lat):
    B, S, _ = P.shape
    full = lambda shape: pl.BlockSpec(shape, lambda b: (0,) * len(shape))
    return pl.pallas_call(
        functools.partial(_pool_kernel, n_lat=n_lat),
        out_shape=jax.ShapeDtypeStruct((B, S, POOL_WIDTH), _ACT),
        grid=(B,),
        in_specs=[pl.BlockSpec((1, S, POOL_WIDTH), lambda b: (b, 0, COL_POOL // POOL_WIDTH)),
                  full(w_pool.shape), full(s_pool.shape)],
        out_specs=pl.BlockSpec((1, S, POOL_WIDTH), lambda b: (b, 0, 0)),
        compiler_params=_params("parallel"),
        name="multiscale_pool",
    )(P, w_pool, s_pool)


def _merge_kernel(a_ref, b_ref, c_ref, ga_ref, gb_ref, gc_ref, x_ref, g1b, g1c,
                  wa_ref, wb_ref, wc_ref, wo_ref, o_ref, *, n_lat, tm):
    def gate(ref):
        return jax.nn.sigmoid(ref[0].astype(F32))
    y = gate(ga_ref) * jnp.dot(a_ref[0], wa_ref[...], preferred_element_type=F32)
    y = y + gate(gb_ref) * jnp.dot(b_ref[0], wb_ref[...], preferred_element_type=F32)
    y = y + gate(gc_ref) * jnp.dot(c_ref[0], wc_ref[...], preferred_element_type=F32)
    out = _dot(y, wo_ref[...])
    rows = pl.program_id(1) * tm + lax.broadcasted_iota(jnp.int32, (tm, 1), 0)
    g1 = jnp.where(rows < n_lat, g1b[0], g1c[0])
    o_ref[0] = x_ref[0] + g1 * out


def _merge(a, b, cp, P, xs, mod, wa, wb, wc, wo, base, n_lat):
    B, S, D = xs.shape
    tm = _tile(S, 256)
    gblk = COL_GATE // D
    tok = lambda w: pl.BlockSpec((1, tm, w), lambda bb, i: (bb, i, 0))
    gate = lambda k: pl.BlockSpec((1, tm, D), lambda bb, i: (bb, i, gblk + k))
    wres = lambda shape: pl.BlockSpec(shape, lambda bb, i: (0, 0), pipeline_mode=pl.Buffered(1))
    return pl.pallas_call(
        functools.partial(_merge_kernel, n_lat=n_lat, tm=tm),
        out_shape=jax.ShapeDtypeStruct((B, S, D), F32),
        grid=(B, S // tm),
        in_specs=[tok(MLA_WIDTH), tok(SGU_WIDTH), tok(POOL_WIDTH), gate(0), gate(1), gate(2), tok(D)]
        + _mod_specs(D, base, B, 2)
        + [wres(wa.shape), wres(wb.shape), wres(wc.shape), wres(wo.shape)],
        out_specs=tok(D),
        compiler_params=_params("parallel", "parallel"),
        name="merge_out",
    )(a, b, cp, P, P, P, xs, mod, mod, wa, wb, wc, wo)


def _pack_rows(o_ref, y):
    rows, half = y.shape[0], y.shape[1] // 2
    nq = half // LANES
    lo = lax.bitcast_convert_type(y[:, :half].astype(jnp.bfloat16).astype(F32), U32)
    hi = lax.bitcast_convert_type(y[:, half:].astype(jnp.bfloat16).astype(F32), U32)
    w = hi | (lo >> 16)
    for q in range(nq):
        o_ref[pl.ds(q, rows, stride=nq), :] = w[:, q * LANES:(q + 1) * LANES]


def _unpack_rows(ref, q, rows, nq):
    w = ref[pl.ds(q, rows, stride=nq), :]
    lo = lax.bitcast_convert_type(w << 16, F32)
    hi = lax.bitcast_convert_type(w & jnp.uint32(0xFFFF0000), F32)
    return lo, hi


def _router_kernel(x_ref, shb, shc, scb, scc, w_ref, b_ref, h_ref, r_ref, cnt_ref, *, n_lat, tm):
    first = (pl.program_id(0) == 0) & (pl.program_id(1) == 0)

    @pl.when(first)
    def _():
        cnt_ref[...] = jnp.zeros_like(cnt_ref)

    h = _norm_mod(x_ref[0], pl.program_id(1) * tm, n_lat, shb[0], scb[0], shc[0], scc[0])
    _pack_rows(h_ref, h)
    logits = jnp.dot(h, w_ref[...], preferred_element_type=F32,
                     precision=lax.Precision.HIGHEST) + b_ref[...]
    lane = lax.broadcasted_iota(jnp.int32, (tm, ROUTE_COLS), 1)
    ninf = -jnp.inf
    lg = jnp.where(lane < N_GROUPS, logits, ninf)
    mg = jnp.max(lg, axis=-1, keepdims=True)
    g_w = 1.0 / jnp.sum(jnp.exp(lg - mg), axis=-1, keepdims=True)
    g_i = jnp.min(jnp.where(lg == mg, lane, ROUTE_COLS), axis=-1, keepdims=True)
    lo = N_GROUPS + g_i * EXPERTS_PER_GROUP
    le = jnp.where((lane >= lo) & (lane < lo + EXPERTS_PER_GROUP), logits, ninf)
    m1 = jnp.max(le, axis=-1, keepdims=True)
    i1 = jnp.min(jnp.where(le == m1, lane, ROUTE_COLS), axis=-1, keepdims=True)
    le2 = jnp.where(lane == i1, ninf, le)
    m2 = jnp.max(le2, axis=-1, keepdims=True)
    i2 = jnp.min(jnp.where(le2 == m2, lane, ROUTE_COLS), axis=-1, keepdims=True)
    e21 = jnp.exp(m2 - m1)
    w1 = 1.0 / (1.0 + e21)
    w2 = e21 * w1
    onehot = jnp.where((lane == i1) | (lane == i2), 1.0, 0.0)
    ti = lax.broadcasted_iota(jnp.int32, (tm, tm), 0)
    tj = lax.broadcasted_iota(jnp.int32, (tm, tm), 1)
    tri = jnp.where(ti > tj, 1.0, 0.0).astype(jnp.bfloat16)
    before = jnp.dot(tri, onehot.astype(jnp.bfloat16), preferred_element_type=F32) + cnt_ref[...]
    r1 = jnp.sum(jnp.where(lane == i1, before, 0.0), axis=-1, keepdims=True)
    r2 = jnp.sum(jnp.where(lane == i2, before, 0.0), axis=-1, keepdims=True)
    cnt_ref[...] += jnp.sum(onehot, axis=0, keepdims=True)
    out = jnp.where(lane == 0, (i1 - N_GROUPS).astype(F32), 0.0)
    out = jnp.where(lane == 1, (i2 - N_GROUPS).astype(F32), out)
    out = jnp.where(lane == 2, g_w * w1, out)
    out = jnp.where(lane == 3, g_w * w2, out)
    out = jnp.where(lane == 4, r1, out)
    out = jnp.where(lane == 5, r2, out)
    r_ref[0] = out


def _router(xs, mod, w_r, b_r, base, n_lat):
    B, S, D = xs.shape
    tm = _tile(S, 512)
    Q2 = D // (2 * LANES)
    nt = S // tm
    full = lambda shape: pl.BlockSpec(shape, lambda b, i: (0,) * len(shape))
    return pl.pallas_call(
        functools.partial(_router_kernel, n_lat=n_lat, tm=tm),
        out_shape=(jax.ShapeDtypeStruct((B * S * Q2, LANES), U32),
                   jax.ShapeDtypeStruct((B, S, ROUTE_COLS), F32),
                   jax.ShapeDtypeStruct((1, ROUTE_COLS), F32)),
        grid=(B, S // tm),
        in_specs=[pl.BlockSpec((1, tm, D), lambda b, i: (b, i, 0))]
        + _mod_specs(D, base, B, 3) + _mod_specs(D, base, B, 4)
        + [full(w_r.shape), full(b_r.shape)],
        out_specs=(pl.BlockSpec((tm * Q2, LANES), lambda b, i: (b * nt + i, 0)),
                   pl.BlockSpec((1, tm, ROUTE_COLS), lambda b, i: (b, i, 0)),
                   pl.BlockSpec((1, ROUTE_COLS), lambda b, i: (0, 0))),
        compiler_params=_params("arbitrary", "arbitrary"),
        name="moe_router",
    )(xs, mod, mod, mod, mod, w_r, b_r)


def _dispatch_kernel(pos_ref, zlo_ref, h_ref, o_ref, zero_ref, sem, *, td, nq):
    step = pl.program_id(0)

    def zero_fill(e):
        start = pl.multiple_of(zlo_ref[e], nq)
        return pltpu.make_async_copy(zero_ref, o_ref.at[pl.ds(start, MOE_TILE * nq)], sem)

    @pl.when(step == 0)
    def _():
        zero_ref[...] = jnp.zeros_like(zero_ref)
        for e in range(2 * N_EXPERTS):
            @pl.when(zlo_ref[e] >= 0)
            def _():
                zero_fill(e).start()
        for e in range(2 * N_EXPERTS):
            @pl.when(zlo_ref[e] >= 0)
            def _():
                zero_fill(e).wait()

    base = step * td

    def row_copy(r, k):
        src = h_ref.at[pl.ds(pl.multiple_of(r * nq, nq), nq)]
        dst = o_ref.at[pl.ds(pl.multiple_of(pos_ref[2 * (base + r) + k], nq), nq)]
        return pltpu.make_async_copy(src, dst, sem)

    def issue(r, carry):
        row_copy(r, 0).start()
        row_copy(r, 1).start()
        return carry

    def drain(r, carry):
        row_copy(r, 0).wait()
        row_copy(r, 1).wait()
        return carry

    lax.fori_loop(0, td, issue, 0)
    lax.fori_loop(0, td, drain, 0)


def _dispatch(pos, zlo, h2p, T, n_rows):
    nq = h2p.shape[0] // T
    td = _tile(T, 512, SUBLANES)
    return pl.pallas_call(
        functools.partial(_dispatch_kernel, td=td, nq=nq),
        out_shape=jax.ShapeDtypeStruct((n_rows * nq, LANES), h2p.dtype),
        grid_spec=pltpu.PrefetchScalarGridSpec(
            num_scalar_prefetch=2,
            grid=(T // td,),
            in_specs=[pl.BlockSpec((td * nq, LANES), lambda i, pos, zlo: (i, 0))],
            out_specs=pl.BlockSpec(memory_space=pl.ANY),
            scratch_shapes=[pltpu.VMEM((MOE_TILE * nq, LANES), h2p.dtype), pltpu.SemaphoreType.DMA(())],
        ),
        compiler_params=pltpu.CompilerParams(dimension_semantics=("arbitrary",),
                                             vmem_limit_bytes=VMEM_LIMIT, has_side_effects=True),
        name="moe_dispatch",
    )(pos, zlo, h2p)


def _expert_kernel(te_ref, nv_ref, x_ref, wg_ref, wu_ref, wd_ref, o_ref, wgu_s, wd_s, *, nq):
    j = pl.program_id(0)
    used = j < nv_ref[0]
    new_expert = (j == 0) | (te_ref[j] != te_ref[jnp.maximum(j - 1, 0)])

    @pl.when(used & new_expert)
    def _():
        wgu_s[:, :EXPERT_HIDDEN] = wg_ref[0].astype(wgu_s.dtype)
        wgu_s[:, EXPERT_HIDDEN:] = wu_ref[0].astype(wgu_s.dtype)
        wd_s[...] = wd_ref[0].astype(wd_s.dtype)

    @pl.when(used)
    def _():
        los, his = [], []
        for q in range(nq):
            lo, hi = _unpack_rows(x_ref, q, MOE_TILE, nq)
            los.append(lo.astype(_MXU))
            his.append(hi.astype(_MXU))
        x = jnp.concatenate(los + his, axis=1)
        hgu = jnp.dot(x, wgu_s[...], preferred_element_type=F32)
        hg = hgu[:, :EXPERT_HIDDEN]
        hid = hg * jax.nn.sigmoid(hg) * hgu[:, EXPERT_HIDDEN:]
        _pack_rows(o_ref, _dot(hid, wd_s[...]))

    @pl.when(jnp.logical_not(used))
    def _():
        o_ref[...] = jnp.zeros_like(o_ref)


def _experts(tile_e, nvalid, xsorted, wg, wu, wd):
    E, D, Hd = wg.shape
    nq = D // (2 * LANES)
    NT = xsorted.shape[0] // (MOE_TILE * nq)
    return pl.pallas_call(
        functools.partial(_expert_kernel, nq=nq),
        out_shape=jax.ShapeDtypeStruct(xsorted.shape, U32),
        grid_spec=pltpu.PrefetchScalarGridSpec(
            num_scalar_prefetch=2,
            grid=(NT,),
            in_specs=[pl.BlockSpec((MOE_TILE * nq, LANES), lambda j, te, nv: (j, 0)),
                      pl.BlockSpec((1, D, Hd), lambda j, te, nv: (te[j], 0, 0)),
                      pl.BlockSpec((1, D, Hd), lambda j, te, nv: (te[j], 0, 0)),
                      pl.BlockSpec((1, Hd, D), lambda j, te, nv: (te[j], 0, 0))],
            out_specs=pl.BlockSpec((MOE_TILE * nq, LANES), lambda j, te, nv: (j, 0)),
            scratch_shapes=[pltpu.VMEM((D, 2 * Hd), _MXU), pltpu.VMEM((Hd, D), _MXU)],
        ),
        compiler_params=_params("arbitrary"),
        name="moe_experts",
    )(tile_e, nvalid, xsorted, wg, wu, wd)


def _combine_kernel(pos_ref, y_ref, x_ref, r_ref, g2b, g2c, o_ref, buf0, buf1, sem, *, n_lat, tc, S, nq):
    b = pl.program_id(0)
    i = pl.program_id(1)
    base = b * S + i * tc

    bufs = (buf0, buf1)

    def row_copy(r, k):
        src = y_ref.at[pl.ds(pl.multiple_of(pos_ref[2 * (base + r) + k], nq), nq)]
        dst = bufs[k].at[pl.ds(pl.multiple_of(r * nq, nq), nq)]
        return pltpu.make_async_copy(src, dst, sem)

    def issue(r, carry):
        row_copy(r, 0).start()
        row_copy(r, 1).start()
        return carry

    def drain(r, carry):
        row_copy(r, 0).wait()
        row_copy(r, 1).wait()
        return carry

    lax.fori_loop(0, tc, issue, 0)
    lax.fori_loop(0, tc, drain, 0)
    r = r_ref[0]
    c1, c2 = r[:, 2:3], r[:, 3:4]
    rows = i * tc + lax.broadcasted_iota(jnp.int32, (tc, 1), 0)
    lat = rows < n_lat
    half = x_ref.shape[2] // 2
    for q in range(nq):
        lo0, hi0 = _unpack_rows(buf0, q, tc, nq)
        lo1, hi1 = _unpack_rows(buf1, q, tc, nq)
        for col, y0, y1 in ((q * LANES, lo0, lo1), (half + q * LANES, hi0, hi1)):
            cs = slice(col, col + LANES)
            g2 = jnp.where(lat, g2b[0, :, cs], g2c[0, :, cs])
            o_ref[0, :, cs] = x_ref[0, :, cs] + g2 * (c1 * y0 + c2 * y1)


def _combine(pos, ysorted, xs, route, mod, base, n_lat):
    B, S, D = xs.shape
    nq = D // (2 * LANES)
    tc = _tile(S, 256, SUBLANES)
    return pl.pallas_call(
        functools.partial(_combine_kernel, n_lat=n_lat, tc=tc, S=S, nq=nq),
        out_shape=jax.ShapeDtypeStruct(xs.shape, F32),
        grid_spec=pltpu.PrefetchScalarGridSpec(
            num_scalar_prefetch=1,
            grid=(B, S // tc),
            in_specs=[pl.BlockSpec(memory_space=pl.ANY),
                      pl.BlockSpec((1, tc, D), lambda b, i, pos: (b, i, 0)),
                      pl.BlockSpec((1, tc, ROUTE_COLS), lambda b, i, pos: (b, i, 0))]
            + _mod_specs(D, base, B, 5),
            out_specs=pl.BlockSpec((1, tc, D), lambda b, i, pos: (b, i, 0)),
            scratch_shapes=[pltpu.VMEM((tc * nq, LANES), U32), pltpu.VMEM((tc * nq, LANES), U32),
                            pltpu.SemaphoreType.DMA(())],
        ),
        compiler_params=_params("arbitrary", "arbitrary"),
        name="moe_combine",
    )(pos, ysorted, xs, route, mod, mod)


def _swap_perm():
    j = np.arange(QK_ROPE)
    a, h, f = j // ROPE_AXIS_DIM, (j // ROPE_FREQS) % 2, j % ROPE_FREQS
    return a * ROPE_AXIS_DIM + (1 - h) * ROPE_FREQS + f


def _rope_tables(n_lat, S):
    t = jnp.arange(n_lat)
    row = (t // GRID_W).astype(F32)
    col = (t % GRID_W).astype(F32)
    inv = ROPE_THETA ** (-jnp.arange(0, ROPE_AXIS_DIM, 2, dtype=F32) / ROPE_AXIS_DIM)
    a0, a1 = row[:, None] * inv, col[:, None] * inv
    c64 = jnp.concatenate([jnp.cos(a0), jnp.cos(a0), jnp.cos(a1), jnp.cos(a1)], axis=1)
    s64 = jnp.concatenate([-jnp.sin(a0), jnp.sin(a0), -jnp.sin(a1), jnp.sin(a1)], axis=1)
    nc = S - n_lat
    c64 = jnp.concatenate([c64, jnp.ones((nc, QK_ROPE), F32)], axis=0)
    s64 = jnp.concatenate([s64, jnp.zeros((nc, QK_ROPE), F32)], axis=0)
    ct = jnp.concatenate([jnp.ones((S, QK_NOPE), F32), c64, jnp.zeros((S, QK_ROPE), F32)], axis=1)
    st = jnp.concatenate([jnp.zeros((S, QK_NOPE), F32), s64, jnp.zeros((S, QK_ROPE), F32)], axis=1)
    return ct, st


def _head_gains(g, perm):
    z = jnp.zeros((QK_ROPE,), F32)
    ga = jnp.concatenate([g[:QK_NOPE], g[QK_NOPE:], z])
    gb = jnp.concatenate([jnp.zeros((QK_NOPE,), F32), g[QK_NOPE:][perm], z])
    return jnp.stack([ga, gb], axis=0)


def _prep_w_in(w, D, perm):
    o = np.cumsum((0, Q_LORA, KV_LORA, QK_ROPE, 2 * SGU_WIDTH, POOL_WIDTH, 3 * D))
    wq, wkv, wkr, wsgu, wpool, wgate = (w[:, o[k]:o[k + 1]] for k in range(6))
    pad = jnp.zeros((D, MLA_BLOCK - Q_LORA - KV_LORA - 2 * QK_ROPE), w.dtype)
    return jnp.concatenate([wsgu, wpool, wq, wkv, pad, wkr, wkr[:, perm], wgate], axis=1).astype(_MXU)


def _slots(route, counts, T, nq):
    e = route[:, 0:2].astype(jnp.int32)
    rank = route[:, 4:6].astype(jnp.int32)
    cnt = counts[0, N_GROUPS:N_GROUPS + N_EXPERTS].astype(jnp.int32)
    padded = (cnt + MOE_TILE - 1) // MOE_TILE * MOE_TILE
    end = jnp.cumsum(padded)
    off = end - padded
    own = e[:, :, None] == jnp.arange(N_EXPERTS, dtype=jnp.int32)
    pos = rank + jnp.sum(jnp.where(own, off, 0), axis=-1)
    n_rows = 2 * T + N_EXPERTS * MOE_TILE
    NT = n_rows // MOE_TILE
    nvalid = (end[-1] // MOE_TILE).astype(jnp.int32)
    tiles = jnp.minimum(jnp.arange(NT, dtype=jnp.int32), nvalid - 1)
    tile_e = jnp.sum((end[None, :] <= (tiles * MOE_TILE)[:, None]).astype(jnp.int32), axis=1)
    tail = nvalid + jnp.arange(N_EXPERTS, dtype=jnp.int32)
    zlo = jnp.concatenate([jnp.where(cnt > 0, (end - MOE_TILE) * nq, -1),
                           jnp.where(tail < NT, tail * (MOE_TILE * nq), -1)]).astype(jnp.int32)
    return (pos.reshape(-1) * nq).astype(jnp.int32), zlo, tile_e, nvalid.reshape(1), n_rows


def kernel(x, c, ctx, c_ctx, w_ada, b_ada, w_in, g_cq, g_ckv, w_uq, w_ukv, g_q, g_k, w_sgu, b_sgu, g_sgu,
           w_pool, s_pool, w_ao, w_bo, w_co, w_out, w_rg, b_rg, w_re, b_re, w_e_gate, w_e_up, w_e_down):
    B, n, D = x.shape
    nc = ctx.shape[1]
    S = n + nc
    T = B * S
    L = w_ada.shape[0]
    perm = _swap_perm()

    R = -(-(B + 1) // SUBLANES) * SUBLANES
    cc = jnp.concatenate([c, c_ctx[None], jnp.zeros((R - B - 1, D), F32)], axis=0)
    mod = _ada(cc, w_ada, b_ada)
    mod2 = mod.reshape(L * R, 1, 6 * D)

    ct, st = _rope_tables(n, S)
    xs = jnp.concatenate([x, ctx], axis=1)

    for i in range(L):
        base = i * R
        w_in_r = _prep_w_in(w_in[i], D, perm)
        wq = w_uq[i].reshape(Q_LORA, MLA_HEADS, QK_HEAD)
        wq = jnp.concatenate([wq, wq[:, :, QK_NOPE:][:, :, perm]], axis=2).reshape(Q_LORA, -1).astype(_MXU)
        wkv = w_ukv[i].astype(_MXU)
        bias = jnp.repeat(b_sgu[i].T, SGU_GROUP_CH, axis=1)
        w_r = jnp.concatenate([w_rg[i], jnp.moveaxis(w_re[i], 0, 1).reshape(D, N_EXPERTS),
                               jnp.zeros((D, ROUTE_COLS - N_GROUPS - N_EXPERTS), F32)], axis=1)
        b_r = jnp.concatenate([b_rg[i], b_re[i].reshape(-1),
                               jnp.zeros((ROUTE_COLS - N_GROUPS - N_EXPERTS,), F32)])[None]

        P = _inproj(xs, mod2, w_in_r, base, n)
        q, k, v = _mla_front(P, wq, wkv, g_cq[i][None], g_ckv[i][None],
                             _head_gains(g_q[i], perm), _head_gains(g_k[i], perm), ct, st)
        a = _attention(q, k, v, n)
        bb = _sgu(P, g_sgu[i][None], w_sgu[i].astype(_MXU), bias)
        cp = _pool(P, w_pool[i].astype(_MXU), s_pool[i][None], n)
        xs = _merge(a, bb, cp, P, xs, mod2, w_ao[i].astype(_MXU), w_bo[i].astype(_MXU),
                    w_co[i].astype(_MXU), w_out[i].astype(_MXU), base, n)

        h2, route, counts = _router(xs, mod2, w_r, b_r, base, n)
        pos, zlo, tile_e, nvalid, n_rows = _slots(route.reshape(T, ROUTE_COLS), counts, T, D // (2 * LANES))
        xsorted = _dispatch(pos, zlo, h2, T, n_rows)
        ysorted = _experts(tile_e, nvalid, xsorted, w_e_gate[i], w_e_up[i], w_e_down[i])
        xs = _combine(pos, ysorted, xs, route, mod2, base, n)

    return xs[:, :n, :]
```

```python
import functools
import math

import jax
import jax.numpy as jnp
import numpy as np
from jax import lax
from jax.experimental import pallas as pl
from jax.experimental.pallas import tpu as pltpu

GRID_W = 64
EPS = 1e-6
MLA_HEADS = 8
QK_NOPE = 128
QK_ROPE = 64
V_HEAD = 128
QK_HEAD = QK_NOPE + QK_ROPE
Q_LORA = 512
KV_LORA = 256
MLA_WIDTH = MLA_HEADS * V_HEAD
ROPE_THETA = 10000.0
ROPE_AXIS_DIM = QK_ROPE // 2
ROPE_FREQS = ROPE_AXIS_DIM // 2
ATTN_SCALE = 1.0 / math.sqrt(QK_HEAD)
LOG2E = math.log2(math.e)
SGU_GROUPS = 8
SGU_CHUNK = 128
SGU_GROUP_CH = 128
SGU_WIDTH = SGU_GROUPS * SGU_GROUP_CH
POOL_WINDOWS = (2, 4, 8, 16)
POOL_GROUP_CH = 256
POOL_WIDTH = len(POOL_WINDOWS) * POOL_GROUP_CH
N_GROUPS = 4
EXPERTS_PER_GROUP = 8
N_EXPERTS = N_GROUPS * EXPERTS_PER_GROUP
EXPERT_HIDDEN = 512

LANES = 128
SUBLANES = 8
HEAD_PAD = 256
VMEM_LIMIT = 56 * 1024 * 1024

MLA_BLOCK = 1024
COL_SGU = 0
COL_POOL = 2 * SGU_WIDTH
COL_MLA = COL_POOL + POOL_WIDTH
COL_GATE = COL_MLA + MLA_BLOCK

POOL_HALO = 16
ROUTE_COLS = LANES
MOE_TILE = 256
NORM_CHUNK = 32
DMA_LAG = 64

_MXU = jnp.bfloat16
_ACT = jnp.bfloat16
F32 = jnp.float32
U32 = jnp.uint32


def _params(*sem):
    return pltpu.CompilerParams(dimension_semantics=sem, vmem_limit_bytes=VMEM_LIMIT)


def _tile(total, cap, mult=LANES):
    best = None
    for t in range(mult, min(total, cap) + 1, mult):
        if total % t == 0:
            best = t
    assert best is not None, (total, cap, mult)
    return best


def _dot(a, b):
    return jnp.dot(a.astype(_MXU), b.astype(_MXU), preferred_element_type=F32)


def _ada_kernel(c_ref, w_ref, b_ref, o_ref):
    c = c_ref[...]
    s = c * jax.nn.sigmoid(c)
    o_ref[0] = _dot(s, w_ref[0]) + b_ref[0]


def _ada(cc, w_ada, b_ada):
    L, D, N = w_ada.shape
    R = cc.shape[0]
    tn = _tile(N, 1024)
    return pl.pallas_call(
        _ada_kernel,
        out_shape=jax.ShapeDtypeStruct((L, R, N), F32),
        grid=(L, N // tn),
        in_specs=[
            pl.BlockSpec((R, D), lambda l, j: (0, 0)),
            pl.BlockSpec((1, D, tn), lambda l, j: (l, 0, j)),
            pl.BlockSpec((1, 1, tn), lambda l, j: (l, 0, j)),
        ],
        out_specs=pl.BlockSpec((1, R, tn), lambda l, j: (l, 0, j)),
        compiler_params=_params("parallel", "parallel"),
        name="ada_mod",
    )(cc, w_ada, b_ada.reshape(L, 1, N))


def _norm_mod(x, row0, n_lat, sh_b, sc_b, sh_c, sc_c):
    ms = jnp.mean(x * x, axis=-1, keepdims=True)
    xn = x * lax.rsqrt(ms + EPS)
    rows = row0 + lax.broadcasted_iota(jnp.int32, (x.shape[0], 1), 0)
    lat = rows < n_lat
    sc = jnp.where(lat, sc_b, sc_c)
    sh = jnp.where(lat, sh_b, sh_c)
    return xn * (1.0 + sc) + sh


def _mod_specs(D, base, B, col):
    return [pl.BlockSpec((1, 1, D), lambda *g: (base + g[0], 0, col)),
            pl.BlockSpec((1, 1, D), lambda *g: (base + B, 0, col))]


def _inproj_kernel(x_ref, shb, shc, scb, scc, w_ref, o_ref, h_ref, *, n_lat, tm):
    @pl.when(pl.program_id(2) == 0)
    def _():
        def chunk(c, carry):
            r0 = pl.multiple_of(c * NORM_CHUNK, NORM_CHUNK)
            h = _norm_mod(x_ref[0, pl.ds(r0, NORM_CHUNK), :], pl.program_id(1) * tm + r0, n_lat,
                          shb[0], scb[0], shc[0], scc[0])
            h_ref[pl.ds(r0, NORM_CHUNK), :] = h.astype(h_ref.dtype)
            return carry
        lax.fori_loop(0, tm // NORM_CHUNK, chunk, 0)
    o_ref[0] = jnp.dot(h_ref[...], w_ref[...], preferred_element_type=F32).astype(o_ref.dtype)


def _inproj(xs, mod, w, base, n_lat):
    B, S, D = xs.shape
    NP = w.shape[1]
    tm = _tile(S, 1024)
    tn = _tile(NP, 1024)
    return pl.pallas_call(
        functools.partial(_inproj_kernel, n_lat=n_lat, tm=tm),
        out_shape=jax.ShapeDtypeStruct((B, S, NP), _ACT),
        grid=(B, S // tm, NP // tn),
        in_specs=[pl.BlockSpec((1, tm, D), lambda b, i, j: (b, i, 0))]
        + _mod_specs(D, base, B, 0) + _mod_specs(D, base, B, 1)
        + [pl.BlockSpec((D, tn), lambda b, i, j: (0, j))],
        out_specs=pl.BlockSpec((1, tm, tn), lambda b, i, j: (b, i, j)),
        scratch_shapes=[pltpu.VMEM((tm, D), _MXU)],
        compiler_params=_params("parallel", "parallel", "arbitrary"),
        name="in_proj",
    )(xs, mod, mod, mod, mod, w)


def _rope_half(b, gct, gst):
    return b * gct + pltpu.roll(b, QK_ROPE, axis=1) * gst


def _sumsq_rope(b):
    lane = lax.broadcasted_iota(jnp.int32, (1, LANES), 1)
    return jnp.sum(jnp.where(lane < QK_ROPE, b * b, 0.0), axis=-1, keepdims=True)


def _mla_kernel(p_ref, wq_ref, wkv_ref, gcq_ref, gckv_ref, gq_ref, gk_ref, ct_ref, st_ref,
                q_ref, k_ref, v_ref):
    p = p_ref[0].astype(F32)
    pq = p[:, :Q_LORA]
    pkv = p[:, Q_LORA:Q_LORA + KV_LORA]
    kr = p[:, MLA_BLOCK - LANES:]
    cq = pq * lax.rsqrt(jnp.mean(pq * pq, axis=-1, keepdims=True) + EPS) * gcq_ref[...]
    ckv = pkv * lax.rsqrt(jnp.mean(pkv * pkv, axis=-1, keepdims=True) + EPS) * gckv_ref[...]
    qf = _dot(cq, wq_ref[...])
    kvf = _dot(ckv, wkv_ref[...])
    ct = ct_ref[...]
    st = st_ref[...]
    gq = gq_ref[...]
    gk = gk_ref[...]
    gq_n, q_gct, q_gst = gq[0:1, :QK_NOPE], gq[0:1, QK_NOPE:] * ct, gq[1:2, QK_NOPE:] * st
    gk_n = gk[0:1, :QK_NOPE]
    k_rot = _rope_half(kr, gk[0:1, QK_NOPE:] * ct, gk[1:2, QK_NOPE:] * st)
    kr_ss = _sumsq_rope(kr)
    for h in range(MLA_HEADS):
        lo = h * HEAD_PAD
        qa, qb = qf[:, lo:lo + QK_NOPE], qf[:, lo + QK_NOPE:lo + HEAD_PAD]
        rq = lax.rsqrt((jnp.sum(qa * qa, axis=-1, keepdims=True) + _sumsq_rope(qb)) * (1.0 / QK_HEAD) + EPS)
        rq = rq * (ATTN_SCALE * LOG2E)
        q_ref[0, :, lo:lo + QK_NOPE] = (qa * gq_n * rq).astype(q_ref.dtype)
        q_ref[0, :, lo + QK_NOPE:lo + HEAD_PAD] = (_rope_half(qb, q_gct, q_gst) * rq).astype(q_ref.dtype)
        ka = kvf[:, lo:lo + QK_NOPE]
        rk = lax.rsqrt((jnp.sum(ka * ka, axis=-1, keepdims=True) + kr_ss) * (1.0 / QK_HEAD) + EPS)
        k_ref[0, :, lo:lo + QK_NOPE] = (ka * gk_n * rk).astype(k_ref.dtype)
        k_ref[0, :, lo + QK_NOPE:lo + HEAD_PAD] = (k_rot * rk).astype(k_ref.dtype)
        v_ref[0, :, h * V_HEAD:(h + 1) * V_HEAD] = kvf[:, lo + QK_NOPE:lo + HEAD_PAD].astype(v_ref.dtype)


def _mla_front(P, wq, wkv, gcq, gckv, gq2, gk2, ct, st):
    B, S, _ = P.shape
    tm = _tile(S, 512)
    HW = MLA_HEADS * HEAD_PAD
    cblk = COL_MLA // MLA_BLOCK
    full = lambda shape: pl.BlockSpec(shape, lambda b, i: (0,) * len(shape))
    return pl.pallas_call(
        _mla_kernel,
        out_shape=(jax.ShapeDtypeStruct((B, S, HW), _ACT),
                   jax.ShapeDtypeStruct((B, S, HW), _ACT),
                   jax.ShapeDtypeStruct((B, S, MLA_WIDTH), _ACT)),
        grid=(B, S // tm),
        in_specs=[pl.BlockSpec((1, tm, MLA_BLOCK), lambda b, i: (b, i, cblk)),
                  full(wq.shape), full(wkv.shape), full(gcq.shape), full(gckv.shape),
                  full(gq2.shape), full(gk2.shape),
                  pl.BlockSpec((tm, LANES), lambda b, i: (i, QK_NOPE // LANES)),
                  pl.BlockSpec((tm, LANES), lambda b, i: (i, QK_NOPE // LANES))],
        out_specs=(pl.BlockSpec((1, tm, HW), lambda b, i: (b, i, 0)),
                   pl.BlockSpec((1, tm, HW), lambda b, i: (b, i, 0)),
                   pl.BlockSpec((1, tm, MLA_WIDTH), lambda b, i: (b, i, 0))),
        compiler_params=_params("parallel", "parallel"),
        name="mla_front",
    )(P, wq, wkv, gcq, gckv, gq2, gk2, ct, st)


def _attn_kernel(q_ref, k_ref, v_ref, o_ref, *, n_lat, tq, heads):
    def run(lo):
        nk = k_ref.shape[1] - lo
        ones = (lax.broadcasted_iota(jnp.int32, (nk, V_HEAD), 1) == 0).astype(v_ref.dtype)
        for h in range(heads):
            q = q_ref[0, :, h * HEAD_PAD:(h + 1) * HEAD_PAD]
            k = k_ref[0, lo:, h * HEAD_PAD:(h + 1) * HEAD_PAD]
            v = jnp.concatenate([v_ref[0, lo:, h * V_HEAD:(h + 1) * V_HEAD], ones], axis=1)
            s = lax.dot_general(q, k, (((1,), (1,)), ((), ())), preferred_element_type=F32)
            p = jnp.exp2(s - jnp.max(s, axis=-1, keepdims=True))
            o = jnp.dot(p.astype(v.dtype), v, preferred_element_type=F32)
            o_ref[0, :, h * V_HEAD:(h + 1) * V_HEAD] = (
                o[:, :V_HEAD] / o[:, V_HEAD:V_HEAD + 1]).astype(o_ref.dtype)

    is_lat = pl.program_id(2) * tq < n_lat

    @pl.when(is_lat)
    def _():
        run(0)

    @pl.when(jnp.logical_not(is_lat))
    def _():
        run(n_lat)


ATTN_HEADS_PER_STEP = 4


def _attention(q, k, v, n_lat):
    B, S, _ = q.shape
    tq = _tile(math.gcd(n_lat, S - n_lat), 256)
    G = ATTN_HEADS_PER_STEP
    return pl.pallas_call(
        functools.partial(_attn_kernel, n_lat=n_lat, tq=tq, heads=G),
        out_shape=jax.ShapeDtypeStruct((B, S, MLA_WIDTH), _ACT),
        grid=(B, MLA_HEADS // G, S // tq),
        in_specs=[pl.BlockSpec((1, tq, G * HEAD_PAD), lambda b, h, i: (b, i, h)),
                  pl.BlockSpec((1, S, G * HEAD_PAD), lambda b, h, i: (b, 0, h)),
                  pl.BlockSpec((1, S, G * V_HEAD), lambda b, h, i: (b, 0, h))],
        out_specs=pl.BlockSpec((1, tq, G * V_HEAD), lambda b, h, i: (b, i, h)),
        compiler_params=_params("parallel", "parallel", "arbitrary"),
        name="attention",
    )(q, k, v)


def _gelu_tanh(x):
    return 0.5 * x * (1.0 + jnp.tanh(math.sqrt(2.0 / math.pi) * (x + 0.044715 * (x * x * x))))


def _sgu_kernel(p_ref, g_ref, w_ref, b_ref, o_ref, *, tm):
    gp = _gelu_tanh(p_ref[0].astype(F32))
    u = gp[:, :SGU_WIDTH]
    v = gp[:, SGU_WIDTH:]
    v = v * lax.rsqrt(jnp.mean(v * v, axis=-1, keepdims=True) + EPS) * g_ref[...]
    vb = v.astype(_MXU)
    bias = b_ref[...]
    for c in range(tm // SGU_CHUNK):
        rs = slice(c * SGU_CHUNK, (c + 1) * SGU_CHUNK)
        for g in range(SGU_GROUPS):
            cs = slice(g * SGU_GROUP_CH, (g + 1) * SGU_GROUP_CH)
            mix = jnp.dot(w_ref[g], vb[rs, cs], preferred_element_type=F32) + bias[:, cs]
            o_ref[0, rs, cs] = (u[rs, cs] * mix).astype(o_ref.dtype)


def _sgu(P, g_sgu, w_sgu, bias):
    B, S, _ = P.shape
    tm = _tile(S, 512)
    full = lambda shape: pl.BlockSpec(shape, lambda b, i: (0,) * len(shape))
    return pl.pallas_call(
        functools.partial(_sgu_kernel, tm=tm),
        out_shape=jax.ShapeDtypeStruct((B, S, SGU_WIDTH), _ACT),
        grid=(B, S // tm),
        in_specs=[pl.BlockSpec((1, tm, 2 * SGU_WIDTH), lambda b, i: (b, i, COL_SGU // (2 * SGU_WIDTH))),
                  full(g_sgu.shape), full(w_sgu.shape), full(bias.shape)],
        out_specs=pl.BlockSpec((1, tm, SGU_WIDTH), lambda b, i: (b, i, 0)),
        compiler_params=_params("parallel", "parallel"),
        name="spatial_gating",
    )(P, g_sgu, w_sgu, bias)


def _window_sum(x, g):
    n, C = x.shape
    z = jnp.zeros((POOL_HALO, C), F32)
    xe = jnp.concatenate([z, x, z], axis=0)
    ne = n + 2 * POOL_HALO
    w = xe + pltpu.roll(xe, 1, axis=0)
    for s in range(g):
        k = 1 << s
        w = pltpu.roll(w, k, axis=0) + pltpu.roll(w, ne - k, axis=0)
    return w[POOL_HALO:POOL_HALO + n]


def _pool_kernel(p_ref, w_ref, s_ref, o_ref, *, n_lat):
    S = p_ref.shape[1]
    for lo, n in ((0, n_lat), (n_lat, S - n_lat)):
        t = lax.broadcasted_iota(jnp.int32, (n, 1), 0)
        for g, win in enumerate(POOL_WINDOWS):
            cs = slice(g * POOL_GROUP_CH, (g + 1) * POOL_GROUP_CH)
            x = p_ref[0, lo:lo + n, cs].astype(F32)
            half = win // 2
            cnt = (jnp.minimum(t + half, n) - jnp.maximum(t - half, 0)).astype(F32)
            m = _window_sum(x, g) / cnt - x
            y = _dot(m, w_ref[g]) * s_ref[:, cs]
            o_ref[0, lo:lo + n, cs] = y.astype(o_ref.dtype)


def _pool(P, w_pool, s_pool, n_lat):
    B, S, _ = P.shape
    full = lambda shape: pl.BlockSpec(shape, lambda b: (0,) * len(shape))
    return pl.pallas_call(
        functools.partial(_pool_kernel, n_lat=n_lat),
        out_shape=jax.ShapeDtypeStruct((B, S, POOL_WIDTH), _ACT),
        grid=(B,),
        in_specs=[pl.BlockSpec((1, S, POOL_WIDTH), lambda b: (b, 0, COL_POOL // POOL_WIDTH)),
                  full(w_pool.shape), full(s_pool.shape)],
        out_specs=pl.BlockSpec((1, S, POOL_WIDTH), lambda b: (b, 0, 0)),
        compiler_params=_params("parallel"),
        name="multiscale_pool",
    )(P, w_pool, s_pool)


def _merge_kernel(a_ref, b_ref, c_ref, ga_ref, gb_ref, gc_ref, x_ref, g1b, g1c,
                  wa_ref, wb_ref, wc_ref, wo_ref, o_ref, *, n_lat, tm):
    def gate(ref):
        return jax.nn.sigmoid(ref[0].astype(F32))
    y = gate(ga_ref) * jnp.dot(a_ref[0], wa_ref[...], preferred_element_type=F32)
    y = y + gate(gb_ref) * jnp.dot(b_ref[0], wb_ref[...], preferred_element_type=F32)
    y = y + gate(gc_ref) * jnp.dot(c_ref[0], wc_ref[...], preferred_element_type=F32)
    out = _dot(y, wo_ref[...])
    rows = pl.program_id(1) * tm + lax.broadcasted_iota(jnp.int32, (tm, 1), 0)
    g1 = jnp.where(rows < n_lat, g1b[0], g1c[0])
    o_ref[0] = x_ref[0] + g1 * out


def _merge(a, b, cp, P, xs, mod, wa, wb, wc, wo, base, n_lat):
    B, S, D = xs.shape
    tm = _tile(S, 256)
    gblk = COL_GATE // D
    tok = lambda w: pl.BlockSpec((1, tm, w), lambda bb, i: (bb, i, 0))
    gate = lambda k: pl.BlockSpec((1, tm, D), lambda bb, i: (bb, i, gblk + k))
    wres = lambda shape: pl.BlockSpec(shape, lambda bb, i: (0, 0), pipeline_mode=pl.Buffered(1))
    return pl.pallas_call(
        functools.partial(_merge_kernel, n_lat=n_lat, tm=tm),
        out_shape=jax.ShapeDtypeStruct((B, S, D), F32),
        grid=(B, S // tm),
        in_specs=[tok(MLA_WIDTH), tok(SGU_WIDTH), tok(POOL_WIDTH), gate(0), gate(1), gate(2), tok(D)]
        + _mod_specs(D, base, B, 2)
        + [wres(wa.shape), wres(wb.shape), wres(wc.shape), wres(wo.shape)],
        out_specs=tok(D),
        compiler_params=_params("parallel", "parallel"),
        name="merge_out",
    )(a, b, cp, P, P, P, xs, mod, mod, wa, wb, wc, wo)


def _pack_rows(o_ref, y):
    rows, half = y.shape[0], y.shape[1] // 2
    nq = half // LANES
    lo = lax.bitcast_convert_type(y[:, :half].astype(jnp.bfloat16).astype(F32), U32)
    hi = lax.bitcast_convert_type(y[:, half:].astype(jnp.bfloat16).astype(F32), U32)
    w = hi | (lo >> 16)
    for q in range(nq):
        o_ref[pl.ds(q, rows, stride=nq), :] = w[:, q * LANES:(q + 1) * LANES]


def _unpack_rows(ref, q, rows, nq):
    w = ref[pl.ds(q, rows, stride=nq), :]
    lo = lax.bitcast_convert_type(w << 16, F32)
    hi = lax.bitcast_convert_type(w & jnp.uint32(0xFFFF0000), F32)
    return lo, hi


def _router_kernel(x_ref, shb, shc, scb, scc, w_ref, b_ref, h_ref, r_ref, cnt_ref, *, n_lat, tm):
    first = (pl.program_id(0) == 0) & (pl.program_id(1) == 0)

    @pl.when(first)
    def _():
        cnt_ref[...] = jnp.zeros_like(cnt_ref)

    h = _norm_mod(x_ref[0], pl.program_id(1) * tm, n_lat, shb[0], scb[0], shc[0], scc[0])
    _pack_rows(h_ref, h)
    h_hi = h.astype(jnp.bfloat16)
    h_lo = (h - h_hi.astype(F32)).astype(jnp.bfloat16)
    t = jnp.dot(h_hi, w_ref[...], preferred_element_type=F32)
    logits = (t[:, :ROUTE_COLS] + t[:, ROUTE_COLS:]
              + jnp.dot(h_lo, w_ref[:, :ROUTE_COLS], preferred_element_type=F32) + b_ref[...])
    lane = lax.broadcasted_iota(jnp.int32, (tm, ROUTE_COLS), 1)
    ninf = -jnp.inf
    lg = jnp.where(lane < N_GROUPS, logits, ninf)
    mg = jnp.max(lg, axis=-1, keepdims=True)
    g_w = 1.0 / jnp.sum(jnp.exp(lg - mg), axis=-1, keepdims=True)
    g_i = jnp.min(jnp.where(lg == mg, lane, ROUTE_COLS), axis=-1, keepdims=True)
    lo = N_GROUPS + g_i * EXPERTS_PER_GROUP
    le = jnp.where((lane >= lo) & (lane < lo + EXPERTS_PER_GROUP), logits, ninf)
    m1 = jnp.max(le, axis=-1, keepdims=True)
    i1 = jnp.min(jnp.where(le == m1, lane, ROUTE_COLS), axis=-1, keepdims=True)
    le2 = jnp.where(lane == i1, ninf, le)
    m2 = jnp.max(le2, axis=-1, keepdims=True)
    i2 = jnp.min(jnp.where(le2 == m2, lane, ROUTE_COLS), axis=-1, keepdims=True)
    e21 = jnp.exp(m2 - m1)
    w1 = 1.0 / (1.0 + e21)
    w2 = e21 * w1
    onehot = jnp.where((lane == i1) | (lane == i2), 1.0, 0.0)
    ti = lax.broadcasted_iota(jnp.int32, (tm, tm), 0)
    tj = lax.broadcasted_iota(jnp.int32, (tm, tm), 1)
    tri = jnp.where(ti > tj, 1.0, 0.0).astype(jnp.bfloat16)
    before = jnp.dot(tri, onehot.astype(jnp.bfloat16), preferred_element_type=F32) + cnt_ref[...]
    r1 = jnp.sum(jnp.where(lane == i1, before, 0.0), axis=-1, keepdims=True)
    r2 = jnp.sum(jnp.where(lane == i2, before, 0.0), axis=-1, keepdims=True)
    cnt_ref[...] += jnp.sum(onehot, axis=0, keepdims=True)
    out = jnp.where(lane == 0, (i1 - N_GROUPS).astype(F32), 0.0)
    out = jnp.where(lane == 1, (i2 - N_GROUPS).astype(F32), out)
    out = jnp.where(lane == 2, g_w * w1, out)
    out = jnp.where(lane == 3, g_w * w2, out)
    out = jnp.where(lane == 4, r1, out)
    out = jnp.where(lane == 5, r2, out)
    r_ref[0] = out


def _router(xs, mod, w_r, b_r, base, n_lat):
    B, S, D = xs.shape
    tm = _tile(S, 512)
    Q2 = D // (2 * LANES)
    nt = S // tm
    full = lambda shape: pl.BlockSpec(shape, lambda b, i: (0,) * len(shape))
    return pl.pallas_call(
        functools.partial(_router_kernel, n_lat=n_lat, tm=tm),
        out_shape=(jax.ShapeDtypeStruct((B * S * Q2, LANES), U32),
                   jax.ShapeDtypeStruct((B, S, ROUTE_COLS), F32),
                   jax.ShapeDtypeStruct((1, ROUTE_COLS), F32)),
        grid=(B, S // tm),
        in_specs=[pl.BlockSpec((1, tm, D), lambda b, i: (b, i, 0))]
        + _mod_specs(D, base, B, 3) + _mod_specs(D, base, B, 4)
        + [full(w_r.shape), full(b_r.shape)],
        out_specs=(pl.BlockSpec((tm * Q2, LANES), lambda b, i: (b * nt + i, 0)),
                   pl.BlockSpec((1, tm, ROUTE_COLS), lambda b, i: (b, i, 0)),
                   pl.BlockSpec((1, ROUTE_COLS), lambda b, i: (0, 0))),
        compiler_params=_params("arbitrary", "arbitrary"),
        name="moe_router",
    )(xs, mod, mod, mod, mod, w_r, b_r)


def _run_row_copies(row_copy, n):
    lag = min(DMA_LAG, n)

    def issue(r, carry):
        row_copy(r, 0).start()
        row_copy(r, 1).start()
        return carry

    def issue_and_retire(r, carry):
        issue(r, carry)
        row_copy(r - lag, 0).wait()
        row_copy(r - lag, 1).wait()
        return carry

    def retire(r, carry):
        row_copy(r, 0).wait()
        row_copy(r, 1).wait()
        return carry

    lax.fori_loop(0, lag, issue, 0)
    lax.fori_loop(lag, n, issue_and_retire, 0)
    lax.fori_loop(n - lag, n, retire, 0)


def _dispatch_kernel(pos_ref, zlo_ref, h_ref, o_ref, zero_ref, sem, *, td, nq):
    step = pl.program_id(0)

    def zero_fill(e):
        start = pl.multiple_of(zlo_ref[e], nq)
        return pltpu.make_async_copy(zero_ref, o_ref.at[pl.ds(start, MOE_TILE * nq)], sem)

    @pl.when(step == 0)
    def _():
        zero_ref[...] = jnp.zeros_like(zero_ref)
        for e in range(2 * N_EXPERTS):
            @pl.when(zlo_ref[e] >= 0)
            def _():
                zero_fill(e).start()
        for e in range(2 * N_EXPERTS):
            @pl.when(zlo_ref[e] >= 0)
            def _():
                zero_fill(e).wait()

    base = step * td

    def row_copy(r, k):
        src = h_ref.at[pl.ds(pl.multiple_of(r * nq, nq), nq)]
        dst = o_ref.at[pl.ds(pl.multiple_of(pos_ref[2 * (base + r) + k], nq), nq)]
        return pltpu.make_async_copy(src, dst, sem)

    _run_row_copies(row_copy, td)


def _dispatch(pos, zlo, h2p, T, n_rows):
    nq = h2p.shape[0] // T
    td = _tile(T, 512, SUBLANES)
    return pl.pallas_call(
        functools.partial(_dispatch_kernel, td=td, nq=nq),
        out_shape=jax.ShapeDtypeStruct((n_rows * nq, LANES), h2p.dtype),
        grid_spec=pltpu.PrefetchScalarGridSpec(
            num_scalar_prefetch=2,
            grid=(T // td,),
            in_specs=[pl.BlockSpec((td * nq, LANES), lambda i, pos, zlo: (i, 0))],
            out_specs=pl.BlockSpec(memory_space=pl.ANY),
            scratch_shapes=[pltpu.VMEM((MOE_TILE * nq, LANES), h2p.dtype), pltpu.SemaphoreType.DMA(())],
        ),
        compiler_params=pltpu.CompilerParams(dimension_semantics=("arbitrary",),
                                             vmem_limit_bytes=VMEM_LIMIT, has_side_effects=True),
        name="moe_dispatch",
    )(pos, zlo, h2p)


def _expert_kernel(te_ref, nv_ref, x_ref, wg_ref, wu_ref, wd_ref, o_ref, wgu_s, wd_s, *, nq):
    j = pl.program_id(0)
    used = j < nv_ref[0]
    new_expert = (j == 0) | (te_ref[j] != te_ref[jnp.maximum(j - 1, 0)])

    @pl.when(used & new_expert)
    def _():
        wgu_s[:, :EXPERT_HIDDEN] = wg_ref[0, 0].astype(wgu_s.dtype)
        wgu_s[:, EXPERT_HIDDEN:] = wu_ref[0, 0].astype(wgu_s.dtype)
        wd_s[...] = wd_ref[0, 0].astype(wd_s.dtype)

    @pl.when(used)
    def _():
        los, his = [], []
        for q in range(nq):
            lo, hi = _unpack_rows(x_ref, q, MOE_TILE, nq)
            los.append(lo.astype(_MXU))
            his.append(hi.astype(_MXU))
        x = jnp.concatenate(los + his, axis=1)
        hgu = jnp.dot(x, wgu_s[...], preferred_element_type=F32)
        hg = hgu[:, :EXPERT_HIDDEN]
        hid = hg * jax.nn.sigmoid(hg) * hgu[:, EXPERT_HIDDEN:]
        _pack_rows(o_ref, _dot(hid, wd_s[...]))

    @pl.when(jnp.logical_not(used))
    def _():
        o_ref[...] = jnp.zeros_like(o_ref)


def _experts(tile_e, nvalid, xsorted, wg, wu, wd, layer):
    _, E, D, Hd = wg.shape
    nq = D // (2 * LANES)
    NT = xsorted.shape[0] // (MOE_TILE * nq)
    return pl.pallas_call(
        functools.partial(_expert_kernel, nq=nq),
        out_shape=jax.ShapeDtypeStruct(xsorted.shape, U32),
        grid_spec=pltpu.PrefetchScalarGridSpec(
            num_scalar_prefetch=2,
            grid=(NT,),
            in_specs=[pl.BlockSpec((MOE_TILE * nq, LANES), lambda j, te, nv: (j, 0)),
                      pl.BlockSpec((1, 1, D, Hd), lambda j, te, nv: (layer, te[j], 0, 0)),
                      pl.BlockSpec((1, 1, D, Hd), lambda j, te, nv: (layer, te[j], 0, 0)),
                      pl.BlockSpec((1, 1, Hd, D), lambda j, te, nv: (layer, te[j], 0, 0))],
            out_specs=pl.BlockSpec((MOE_TILE * nq, LANES), lambda j, te, nv: (j, 0)),
            scratch_shapes=[pltpu.VMEM((D, 2 * Hd), _MXU), pltpu.VMEM((Hd, D), _MXU)],
        ),
        compiler_params=_params("arbitrary"),
        name="moe_experts",
    )(tile_e, nvalid, xsorted, wg, wu, wd)


def _combine_kernel(pos_ref, y_ref, x_ref, r_ref, g2b, g2c, o_ref, buf0, buf1, sem, *, n_lat, tc, S, nq):
    b = pl.program_id(0)
    i = pl.program_id(1)
    base = b * S + i * tc

    bufs = (buf0, buf1)

    def row_copy(r, k):
        src = y_ref.at[pl.ds(pl.multiple_of(pos_ref[2 * (base + r) + k], nq), nq)]
        dst = bufs[k].at[pl.ds(pl.multiple_of(r * nq, nq), nq)]
        return pltpu.make_async_copy(src, dst, sem)

    _run_row_copies(row_copy, tc)
    r = r_ref[0]
    c1, c2 = r[:, 2:3], r[:, 3:4]
    rows = i * tc + lax.broadcasted_iota(jnp.int32, (tc, 1), 0)
    lat = rows < n_lat
    half = x_ref.shape[2] // 2
    for q in range(nq):
        lo0, hi0 = _unpack_rows(buf0, q, tc, nq)
        lo1, hi1 = _unpack_rows(buf1, q, tc, nq)
        for col, y0, y1 in ((q * LANES, lo0, lo1), (half + q * LANES, hi0, hi1)):
            cs = slice(col, col + LANES)
            g2 = jnp.where(lat, g2b[0, :, cs], g2c[0, :, cs])
            o_ref[0, :, cs] = x_ref[0, :, cs] + g2 * (c1 * y0 + c2 * y1)


def _combine(pos, ysorted, xs, route, mod, base, n_lat, rows):
    B, S, D = xs.shape
    nq = D // (2 * LANES)
    tc = _tile(math.gcd(rows, S), 256, SUBLANES)
    return pl.pallas_call(
        functools.partial(_combine_kernel, n_lat=n_lat, tc=tc, S=S, nq=nq),
        out_shape=jax.ShapeDtypeStruct((B, rows, D), F32),
        grid_spec=pltpu.PrefetchScalarGridSpec(
            num_scalar_prefetch=1,
            grid=(B, rows // tc),
            in_specs=[pl.BlockSpec(memory_space=pl.ANY),
                      pl.BlockSpec((1, tc, D), lambda b, i, pos: (b, i, 0)),
                      pl.BlockSpec((1, tc, ROUTE_COLS), lambda b, i, pos: (b, i, 0))]
            + _mod_specs(D, base, B, 5),
            out_specs=pl.BlockSpec((1, tc, D), lambda b, i, pos: (b, i, 0)),
            scratch_shapes=[pltpu.VMEM((tc * nq, LANES), U32), pltpu.VMEM((tc * nq, LANES), U32),
                            pltpu.SemaphoreType.DMA(())],
        ),
        compiler_params=_params("arbitrary", "arbitrary"),
        name="moe_combine",
    )(pos, ysorted, xs, route, mod, mod)


def _swap_perm():
    j = np.arange(QK_ROPE)
    a, h, f = j // ROPE_AXIS_DIM, (j // ROPE_FREQS) % 2, j % ROPE_FREQS
    return a * ROPE_AXIS_DIM + (1 - h) * ROPE_FREQS + f


def _rope_tables(n_lat, S):
    t = jnp.arange(n_lat)
    row = (t // GRID_W).astype(F32)
    col = (t % GRID_W).astype(F32)
    inv = ROPE_THETA ** (-jnp.arange(0, ROPE_AXIS_DIM, 2, dtype=F32) / ROPE_AXIS_DIM)
    a0, a1 = row[:, None] * inv, col[:, None] * inv
    c64 = jnp.concatenate([jnp.cos(a0), jnp.cos(a0), jnp.cos(a1), jnp.cos(a1)], axis=1)
    s64 = jnp.concatenate([-jnp.sin(a0), jnp.sin(a0), -jnp.sin(a1), jnp.sin(a1)], axis=1)
    nc = S - n_lat
    c64 = jnp.concatenate([c64, jnp.ones((nc, QK_ROPE), F32)], axis=0)
    s64 = jnp.concatenate([s64, jnp.zeros((nc, QK_ROPE), F32)], axis=0)
    ct = jnp.concatenate([jnp.ones((S, QK_NOPE), F32), c64, jnp.zeros((S, QK_ROPE), F32)], axis=1)
    st = jnp.concatenate([jnp.zeros((S, QK_NOPE), F32), s64, jnp.zeros((S, QK_ROPE), F32)], axis=1)
    return ct, st


def _head_gains(g, perm):
    z = jnp.zeros((QK_ROPE,), F32)
    ga = jnp.concatenate([g[:QK_NOPE], g[QK_NOPE:], z])
    gb = jnp.concatenate([jnp.zeros((QK_NOPE,), F32), g[QK_NOPE:][perm], z])
    return jnp.stack([ga, gb], axis=0)


def _prep_w_in(w, D, perm):
    o = np.cumsum((0, Q_LORA, KV_LORA, QK_ROPE, 2 * SGU_WIDTH, POOL_WIDTH, 3 * D))
    wq, wkv, wkr, wsgu, wpool, wgate = (w[:, o[k]:o[k + 1]] for k in range(6))
    pad = jnp.zeros((D, MLA_BLOCK - Q_LORA - KV_LORA - 2 * QK_ROPE), w.dtype)
    return jnp.concatenate([wsgu, wpool, wq, wkv, pad, wkr, wkr[:, perm], wgate], axis=1).astype(_MXU)


def _slots(route, counts, T, nq):
    e = route[:, 0:2].astype(jnp.int32)
    rank = route[:, 4:6].astype(jnp.int32)
    cnt = counts[0, N_GROUPS:N_GROUPS + N_EXPERTS].astype(jnp.int32)
    padded = (cnt + MOE_TILE - 1) // MOE_TILE * MOE_TILE
    end = jnp.cumsum(padded)
    off = end - padded
    own = e[:, :, None] == jnp.arange(N_EXPERTS, dtype=jnp.int32)
    pos = rank + jnp.sum(jnp.where(own, off, 0), axis=-1)
    n_rows = 2 * T + N_EXPERTS * MOE_TILE
    NT = n_rows // MOE_TILE
    nvalid = (end[-1] // MOE_TILE).astype(jnp.int32)
    tiles = jnp.minimum(jnp.arange(NT, dtype=jnp.int32), nvalid - 1)
    tile_e = jnp.sum((end[None, :] <= (tiles * MOE_TILE)[:, None]).astype(jnp.int32), axis=1)
    tail = nvalid + jnp.arange(N_EXPERTS, dtype=jnp.int32)
    zlo = jnp.concatenate([jnp.where(cnt > 0, (end - MOE_TILE) * nq, -1),
                           jnp.where(tail < NT, tail * (MOE_TILE * nq), -1)]).astype(jnp.int32)
    return (pos.reshape(-1) * nq).astype(jnp.int32), zlo, tile_e, nvalid.reshape(1), n_rows


def kernel(x, c, ctx, c_ctx, w_ada, b_ada, w_in, g_cq, g_ckv, w_uq, w_ukv, g_q, g_k, w_sgu, b_sgu, g_sgu,
           w_pool, s_pool, w_ao, w_bo, w_co, w_out, w_rg, b_rg, w_re, b_re, w_e_gate, w_e_up, w_e_down):
    B, n, D = x.shape
    nc = ctx.shape[1]
    S = n + nc
    T = B * S
    L = w_ada.shape[0]
    perm = _swap_perm()

    R = -(-(B + 1) // SUBLANES) * SUBLANES
    cc = jnp.concatenate([c, c_ctx[None], jnp.zeros((R - B - 1, D), F32)], axis=0)
    mod = _ada(cc, w_ada, b_ada)
    mod2 = mod.reshape(L * R, 1, 6 * D)

    ct, st = _rope_tables(n, S)
    xs = jnp.concatenate([x, ctx], axis=1)

    for i in range(L):
        base = i * R
        w_in_r = _prep_w_in(w_in[i], D, perm)
        wq = w_uq[i].reshape(Q_LORA, MLA_HEADS, QK_HEAD)
        wq = jnp.concatenate([wq, wq[:, :, QK_NOPE:][:, :, perm]], axis=2).reshape(Q_LORA, -1).astype(_MXU)
        wkv = w_ukv[i].astype(_MXU)
        bias = jnp.repeat(b_sgu[i].T, SGU_GROUP_CH, axis=1)
        w_r = jnp.concatenate([w_rg[i], jnp.moveaxis(w_re[i], 0, 1).reshape(D, N_EXPERTS),
                               jnp.zeros((D, ROUTE_COLS - N_GROUPS - N_EXPERTS), F32)], axis=1)
        b_r = jnp.concatenate([b_rg[i], b_re[i].reshape(-1),
                               jnp.zeros((ROUTE_COLS - N_GROUPS - N_EXPERTS,), F32)])[None]
        w_r_hi = w_r.astype(jnp.bfloat16)
        w_r = jnp.concatenate([w_r_hi, (w_r - w_r_hi.astype(F32)).astype(jnp.bfloat16)], axis=1)

        P = _inproj(xs, mod2, w_in_r, base, n)
        q, k, v = _mla_front(P, wq, wkv, g_cq[i][None], g_ckv[i][None],
                             _head_gains(g_q[i], perm), _head_gains(g_k[i], perm), ct, st)
        a = _attention(q, k, v, n)
        bb = _sgu(P, g_sgu[i][None], w_sgu[i].astype(_MXU), bias)
        cp = _pool(P, w_pool[i].astype(_MXU), s_pool[i][None], n)
        xs = _merge(a, bb, cp, P, xs, mod2, w_ao[i].astype(_MXU), w_bo[i].astype(_MXU),
                    w_co[i].astype(_MXU), w_out[i].astype(_MXU), base, n)

        h2, route, counts = _router(xs, mod2, w_r, b_r, base, n)
        pos, zlo, tile_e, nvalid, n_rows = _slots(route.reshape(T, ROUTE_COLS), counts, T, D // (2 * LANES))
        xsorted = _dispatch(pos, zlo, h2, T, n_rows)
        ysorted = _experts(tile_e, nvalid, xsorted, w_e_gate, w_e_up, w_e_down, i)
        xs = _combine(pos, ysorted, xs, route, mod2, base, n, S if i < L - 1 else n)

    return xs
```

```python
import functools
import math

import jax
import jax.numpy as jnp
import numpy as np
from jax import lax
from jax.experimental import pallas as pl
from jax.experimental.pallas import tpu as pltpu

GRID_W = 64
EPS = 1e-6
MLA_HEADS = 8
QK_NOPE = 128
QK_ROPE = 64
V_HEAD = 128
QK_HEAD = QK_NOPE + QK_ROPE
Q_LORA = 512
KV_LORA = 256
MLA_WIDTH = MLA_HEADS * V_HEAD
ROPE_THETA = 10000.0
ROPE_AXIS_DIM = QK_ROPE // 2
ROPE_FREQS = ROPE_AXIS_DIM // 2
ATTN_SCALE = 1.0 / math.sqrt(QK_HEAD)
LOG2E = math.log2(math.e)
SGU_GROUPS = 8
SGU_CHUNK = 128
SGU_GROUP_CH = 128
SGU_WIDTH = SGU_GROUPS * SGU_GROUP_CH
POOL_WINDOWS = (2, 4, 8, 16)
POOL_GROUP_CH = 256
POOL_WIDTH = len(POOL_WINDOWS) * POOL_GROUP_CH
N_GROUPS = 4
EXPERTS_PER_GROUP = 8
N_EXPERTS = N_GROUPS * EXPERTS_PER_GROUP
EXPERT_HIDDEN = 512

LANES = 128
SUBLANES = 8
HEAD_PAD = 256
VMEM_LIMIT = 56 * 1024 * 1024

MLA_BLOCK = 1024
COL_SGU = 0
COL_POOL = 2 * SGU_WIDTH
COL_MLA = COL_POOL + POOL_WIDTH
COL_GATE = COL_MLA + MLA_BLOCK

POOL_HALO = 16
ROUTE_COLS = LANES
MOE_TILE = 512
NORM_CHUNK = 32

_MXU = jnp.bfloat16
_ACT = jnp.bfloat16
F32 = jnp.float32
U32 = jnp.uint32


def _params(*sem):
    return pltpu.CompilerParams(dimension_semantics=sem, vmem_limit_bytes=VMEM_LIMIT)


def _tile(total, cap, mult=LANES):
    best = None
    for t in range(mult, min(total, cap) + 1, mult):
        if total % t == 0:
            best = t
    assert best is not None, (total, cap, mult)
    return best


def _dot(a, b):
    return jnp.dot(a.astype(_MXU), b.astype(_MXU), preferred_element_type=F32)


def _ada_kernel(c_ref, w_ref, b_ref, o_ref):
    c = c_ref[...]
    s = c * jax.nn.sigmoid(c)
    o_ref[0] = _dot(s, w_ref[0]) + b_ref[0]


def _ada(cc, w_ada, b_ada):
    L, D, N = w_ada.shape
    R = cc.shape[0]
    tn = _tile(N, 1024)
    return pl.pallas_call(
        _ada_kernel,
        out_shape=jax.ShapeDtypeStruct((L, R, N), F32),
        grid=(L, N // tn),
        in_specs=[
            pl.BlockSpec((R, D), lambda l, j: (0, 0)),
            pl.BlockSpec((1, D, tn), lambda l, j: (l, 0, j)),
            pl.BlockSpec((1, 1, tn), lambda l, j: (l, 0, j)),
        ],
        out_specs=pl.BlockSpec((1, R, tn), lambda l, j: (l, 0, j)),
        compiler_params=_params("parallel", "parallel"),
        name="ada_mod",
    )(cc, w_ada, b_ada.reshape(L, 1, N))


def _norm_mod(x, row0, n_lat, sh_b, sc_b, sh_c, sc_c):
    ms = jnp.mean(x * x, axis=-1, keepdims=True)
    xn = x * lax.rsqrt(ms + EPS)
    rows = row0 + lax.broadcasted_iota(jnp.int32, (x.shape[0], 1), 0)
    lat = rows < n_lat
    sc = jnp.where(lat, sc_b, sc_c)
    sh = jnp.where(lat, sh_b, sh_c)
    return xn * (1.0 + sc) + sh


def _mod_specs(D, base, B, col):
    return [pl.BlockSpec((1, 1, D), lambda *g: (base + g[0], 0, col)),
            pl.BlockSpec((1, 1, D), lambda *g: (base + B, 0, col))]


def _inproj_kernel(x_ref, shb, shc, scb, scc, w_ref, o_ref, h_ref, *, n_lat, tm):
    @pl.when(pl.program_id(2) == 0)
    def _():
        def chunk(c, carry):
            r0 = pl.multiple_of(c * NORM_CHUNK, NORM_CHUNK)
            h = _norm_mod(x_ref[0, pl.ds(r0, NORM_CHUNK), :], pl.program_id(1) * tm + r0, n_lat,
                          shb[0], scb[0], shc[0], scc[0])
            h_ref[pl.ds(r0, NORM_CHUNK), :] = h.astype(h_ref.dtype)
            return carry
        lax.fori_loop(0, tm // NORM_CHUNK, chunk, 0)
    o_ref[0] = jnp.dot(h_ref[...], w_ref[...], preferred_element_type=F32).astype(o_ref.dtype)


def _inproj(xs, mod, w, base, n_lat):
    B, S, D = xs.shape
    NP = w.shape[1]
    tm = _tile(S, 1024)
    tn = _tile(NP, 1024)
    return pl.pallas_call(
        functools.partial(_inproj_kernel, n_lat=n_lat, tm=tm),
        out_shape=jax.ShapeDtypeStruct((B, S, NP), _ACT),
        grid=(B, S // tm, NP // tn),
        in_specs=[pl.BlockSpec((1, tm, D), lambda b, i, j: (b, i, 0))]
        + _mod_specs(D, base, B, 0) + _mod_specs(D, base, B, 1)
        + [pl.BlockSpec((D, tn), lambda b, i, j: (0, j))],
        out_specs=pl.BlockSpec((1, tm, tn), lambda b, i, j: (b, i, j)),
        scratch_shapes=[pltpu.VMEM((tm, D), _MXU)],
        compiler_params=_params("parallel", "parallel", "arbitrary"),
        name="in_proj",
    )(xs, mod, mod, mod, mod, w)


def _rope_half(b, gct, gst):
    return b * gct + pltpu.roll(b, QK_ROPE, axis=1) * gst


def _sumsq_rope(b):
    lane = lax.broadcasted_iota(jnp.int32, (1, LANES), 1)
    return jnp.sum(jnp.where(lane < QK_ROPE, b * b, 0.0), axis=-1, keepdims=True)


def _mla_kernel(p_ref, wq_ref, wkv_ref, gcq_ref, gckv_ref, gq_ref, gk_ref, ct_ref, st_ref,
                q_ref, k_ref, v_ref):
    p = p_ref[0].astype(F32)
    pq = p[:, :Q_LORA]
    pkv = p[:, Q_LORA:Q_LORA + KV_LORA]
    kr = p[:, MLA_BLOCK - LANES:]
    cq = pq * lax.rsqrt(jnp.mean(pq * pq, axis=-1, keepdims=True) + EPS) * gcq_ref[...]
    ckv = pkv * lax.rsqrt(jnp.mean(pkv * pkv, axis=-1, keepdims=True) + EPS) * gckv_ref[...]
    qf = _dot(cq, wq_ref[...])
    kvf = _dot(ckv, wkv_ref[...])
    ct = ct_ref[...]
    st = st_ref[...]
    gq = gq_ref[...]
    gk = gk_ref[...]
    gq_n, q_gct, q_gst = gq[0:1, :QK_NOPE], gq[0:1, QK_NOPE:] * ct, gq[1:2, QK_NOPE:] * st
    gk_n = gk[0:1, :QK_NOPE]
    k_rot = _rope_half(kr, gk[0:1, QK_NOPE:] * ct, gk[1:2, QK_NOPE:] * st)
    kr_ss = _sumsq_rope(kr)
    for h in range(MLA_HEADS):
        lo = h * HEAD_PAD
        qa, qb = qf[:, lo:lo + QK_NOPE], qf[:, lo + QK_NOPE:lo + HEAD_PAD]
        rq = lax.rsqrt((jnp.sum(qa * qa, axis=-1, keepdims=True) + _sumsq_rope(qb)) * (1.0 / QK_HEAD) + EPS)
        rq = rq * (ATTN_SCALE * LOG2E)
        q_ref[0, :, lo:lo + QK_NOPE] = (qa * gq_n * rq).astype(q_ref.dtype)
        q_ref[0, :, lo + QK_NOPE:lo + HEAD_PAD] = (_rope_half(qb, q_gct, q_gst) * rq).astype(q_ref.dtype)
        ka = kvf[:, lo:lo + QK_NOPE]
        rk = lax.rsqrt((jnp.sum(ka * ka, axis=-1, keepdims=True) + kr_ss) * (1.0 / QK_HEAD) + EPS)
        k_ref[0, :, lo:lo + QK_NOPE] = (ka * gk_n * rk).astype(k_ref.dtype)
        k_ref[0, :, lo + QK_NOPE:lo + HEAD_PAD] = (k_rot * rk).astype(k_ref.dtype)
        v_ref[0, :, h * V_HEAD:(h + 1) * V_HEAD] = kvf[:, lo + QK_NOPE:lo + HEAD_PAD].astype(v_ref.dtype)


def _mla_front(P, wq, wkv, gcq, gckv, gq2, gk2, ct, st):
    B, S, _ = P.shape
    tm = _tile(S, 512)
    HW = MLA_HEADS * HEAD_PAD
    cblk = COL_MLA // MLA_BLOCK
    full = lambda shape: pl.BlockSpec(shape, lambda b, i: (0,) * len(shape))
    return pl.pallas_call(
        _mla_kernel,
        out_shape=(jax.ShapeDtypeStruct((B, S, HW), _ACT),
                   jax.ShapeDtypeStruct((B, S, HW), _ACT),
                   jax.ShapeDtypeStruct((B, S, MLA_WIDTH), _ACT)),
        grid=(B, S // tm),
        in_specs=[pl.BlockSpec((1, tm, MLA_BLOCK), lambda b, i: (b, i, cblk)),
                  full(wq.shape), full(wkv.shape), full(gcq.shape), full(gckv.shape),
                  full(gq2.shape), full(gk2.shape),
                  pl.BlockSpec((tm, LANES), lambda b, i: (i, QK_NOPE // LANES)),
                  pl.BlockSpec((tm, LANES), lambda b, i: (i, QK_NOPE // LANES))],
        out_specs=(pl.BlockSpec((1, tm, HW), lambda b, i: (b, i, 0)),
                   pl.BlockSpec((1, tm, HW), lambda b, i: (b, i, 0)),
                   pl.BlockSpec((1, tm, MLA_WIDTH), lambda b, i: (b, i, 0))),
        compiler_params=_params("parallel", "parallel"),
        name="mla_front",
    )(P, wq, wkv, gcq, gckv, gq2, gk2, ct, st)


def _attn_kernel(q_ref, k_ref, v_ref, o_ref):
    nk = k_ref.shape[1]
    ones = (lax.broadcasted_iota(jnp.int32, (nk, V_HEAD), 1) == 0).astype(v_ref.dtype)
    for h in range(MLA_HEADS):
        q = q_ref[0, :, h * HEAD_PAD:(h + 1) * HEAD_PAD]
        k = k_ref[0, :, h * HEAD_PAD:(h + 1) * HEAD_PAD]
        v = jnp.concatenate([v_ref[0, :, h * V_HEAD:(h + 1) * V_HEAD], ones], axis=1)
        s = lax.dot_general(q, k, (((1,), (1,)), ((), ())), preferred_element_type=F32)
        p = jnp.exp2(s - jnp.max(s, axis=-1, keepdims=True))
        o = jnp.dot(p.astype(v.dtype), v, preferred_element_type=F32)
        o_ref[0, :, h * V_HEAD:(h + 1) * V_HEAD] = (
            o[:, :V_HEAD] / o[:, V_HEAD:V_HEAD + 1]).astype(o_ref.dtype)


def _attention(q, k, v, n_lat):
    B, S, HW = q.shape
    nc = S - n_lat
    assert n_lat % nc == 0, (n_lat, nc)
    tq = _tile(n_lat, 512)
    a_lat = pl.pallas_call(
        _attn_kernel,
        out_shape=jax.ShapeDtypeStruct((B, n_lat, MLA_WIDTH), _ACT),
        grid=(B, n_lat // tq),
        in_specs=[pl.BlockSpec((1, tq, HW), lambda b, i: (b, i, 0)),
                  pl.BlockSpec((1, S, HW), lambda b, i: (b, 0, 0)),
                  pl.BlockSpec((1, S, MLA_WIDTH), lambda b, i: (b, 0, 0))],
        out_specs=pl.BlockSpec((1, tq, MLA_WIDTH), lambda b, i: (b, i, 0)),
        compiler_params=_params("parallel", "arbitrary"),
        name="attention",
    )(q, k, v)
    cblk = n_lat // nc
    ctx_rows = lambda w: pl.BlockSpec((1, nc, w), lambda b: (b, cblk, 0))
    a_ctx = pl.pallas_call(
        _attn_kernel,
        out_shape=jax.ShapeDtypeStruct((B, nc, MLA_WIDTH), _ACT),
        grid=(B,),
        in_specs=[ctx_rows(HW), ctx_rows(HW), ctx_rows(MLA_WIDTH)],
        out_specs=pl.BlockSpec((1, nc, MLA_WIDTH), lambda b: (b, 0, 0)),
        compiler_params=_params("parallel"),
        name="attention_ctx",
    )(q, k, v)
    return a_lat, a_ctx


def _gelu_tanh(x):
    return 0.5 * x * (1.0 + jnp.tanh(math.sqrt(2.0 / math.pi) * (x + 0.044715 * (x * x * x))))


def _sgu_kernel(p_ref, g_ref, w_ref, b_ref, o_ref, *, tm):
    gp = _gelu_tanh(p_ref[0].astype(F32))
    u = gp[:, :SGU_WIDTH]
    v = gp[:, SGU_WIDTH:]
    v = v * lax.rsqrt(jnp.mean(v * v, axis=-1, keepdims=True) + EPS) * g_ref[...]
    vb = v.astype(_MXU)
    bias = b_ref[...]
    for c in range(tm // SGU_CHUNK):
        rs = slice(c * SGU_CHUNK, (c + 1) * SGU_CHUNK)
        for g in range(SGU_GROUPS):
            cs = slice(g * SGU_GROUP_CH, (g + 1) * SGU_GROUP_CH)
            mix = jnp.dot(w_ref[g], vb[rs, cs], preferred_element_type=F32) + bias[:, cs]
            o_ref[0, rs, cs] = (u[rs, cs] * mix).astype(o_ref.dtype)


def _sgu(P, g_sgu, w_sgu, bias):
    B, S, _ = P.shape
    tm = _tile(S, 512)
    full = lambda shape: pl.BlockSpec(shape, lambda b, i: (0,) * len(shape))
    return pl.pallas_call(
        functools.partial(_sgu_kernel, tm=tm),
        out_shape=jax.ShapeDtypeStruct((B, S, SGU_WIDTH), _ACT),
        grid=(B, S // tm),
        in_specs=[pl.BlockSpec((1, tm, 2 * SGU_WIDTH), lambda b, i: (b, i, COL_SGU // (2 * SGU_WIDTH))),
                  full(g_sgu.shape), full(w_sgu.shape), full(bias.shape)],
        out_specs=pl.BlockSpec((1, tm, SGU_WIDTH), lambda b, i: (b, i, 0)),
        compiler_params=_params("parallel", "parallel"),
        name="spatial_gating",
    )(P, g_sgu, w_sgu, bias)


def _window_sum(x, g):
    n, C = x.shape
    z = jnp.zeros((POOL_HALO, C), F32)
    xe = jnp.concatenate([z, x, z], axis=0)
    ne = n + 2 * POOL_HALO
    w = xe + pltpu.roll(xe, 1, axis=0)
    for s in range(g):
        k = 1 << s
        w = pltpu.roll(w, k, axis=0) + pltpu.roll(w, ne - k, axis=0)
    return w[POOL_HALO:POOL_HALO + n]


def _pool_kernel(p_ref, w_ref, s_ref, o_ref, *, n_lat):
    S = p_ref.shape[1]
    for lo, n in ((0, n_lat), (n_lat, S - n_lat)):
        t = lax.broadcasted_iota(jnp.int32, (n, 1), 0)
        for g, win in enumerate(POOL_WINDOWS):
            cs = slice(g * POOL_GROUP_CH, (g + 1) * POOL_GROUP_CH)
            x = p_ref[0, lo:lo + n, cs].astype(F32)
            half = win // 2
            cnt = (jnp.minimum(t + half, n) - jnp.maximum(t - half, 0)).astype(F32)
            m = _window_sum(x, g) / cnt - x
            y = _dot(m, w_ref[g]) * s_ref[:, cs]
            o_ref[0, lo:lo + n, cs] = y.astype(o_ref.dtype)


def _pool(P, w_pool, s_pool, n_lat):
    B, S, _ = P.shape
    full = lambda shape: pl.BlockSpec(shape, lambda b: (0,) * len(shape))
    return pl.pallas_call(
        functools.partial(_pool_kernel, n_lat=n_lat),
        out_shape=jax.ShapeDtypeStruct((B, S, POOL_WIDTH), _ACT),
        grid=(B,),
        in_specs=[pl.BlockSpec((1, S, POOL_WIDTH), lambda b: (b, 0, COL_POOL // POOL_WIDTH)),
                  full(w_pool.shape), full(s_pool.shape)],
        out_specs=pl.BlockSpec((1, S, POOL_WIDTH), lambda b: (b, 0, 0)),
        compiler_params=_params("parallel"),
        name="multiscale_pool",
    )(P, w_pool, s_pool)


def _merge_kernel(al_ref, ac_ref, b_ref, c_ref, ga_ref, gb_ref, gc_ref, x_ref, g1b, g1c,
                  wa_ref, wb_ref, wc_ref, wo_ref, o_ref, *, n_lat, tm):
    def gate(ref):
        return jax.nn.sigmoid(ref[0].astype(F32))
    a = jnp.where(pl.program_id(1) * tm < n_lat, al_ref[0], ac_ref[0])
    y = gate(ga_ref) * jnp.dot(a, wa_ref[...], preferred_element_type=F32)
    y = y + gate(gb_ref) * jnp.dot(b_ref[0], wb_ref[...], preferred_element_type=F32)
    y = y + gate(gc_ref) * jnp.dot(c_ref[0], wc_ref[...], preferred_element_type=F32)
    out = _dot(y, wo_ref[...])
    rows = pl.program_id(1) * tm + lax.broadcasted_iota(jnp.int32, (tm, 1), 0)
    g1 = jnp.where(rows < n_lat, g1b[0], g1c[0])
    o_ref[0] = x_ref[0] + g1 * out


def _merge(a_lat, a_ctx, b, cp, P, xs, mod, wa, wb, wc, wo, base, n_lat):
    B, S, D = xs.shape
    tm = _tile(math.gcd(n_lat, S - n_lat), 256)
    gblk = COL_GATE // D
    nl = n_lat // tm
    tok = lambda w: pl.BlockSpec((1, tm, w), lambda bb, i: (bb, i, 0))
    att_lat = pl.BlockSpec((1, tm, MLA_WIDTH), lambda bb, i: (bb, jnp.minimum(i, nl - 1), 0))
    att_ctx = pl.BlockSpec((1, tm, MLA_WIDTH), lambda bb, i: (bb, jnp.maximum(i - nl, 0), 0))
    gate = lambda k: pl.BlockSpec((1, tm, D), lambda bb, i: (bb, i, gblk + k))
    wres = lambda shape: pl.BlockSpec(shape, lambda bb, i: (0, 0), pipeline_mode=pl.Buffered(1))
    return pl.pallas_call(
        functools.partial(_merge_kernel, n_lat=n_lat, tm=tm),
        out_shape=jax.ShapeDtypeStruct((B, S, D), F32),
        grid=(B, S // tm),
        in_specs=[att_lat, att_ctx, tok(SGU_WIDTH), tok(POOL_WIDTH), gate(0), gate(1), gate(2), tok(D)]
        + _mod_specs(D, base, B, 2)
        + [wres(wa.shape), wres(wb.shape), wres(wc.shape), wres(wo.shape)],
        out_specs=tok(D),
        compiler_params=_params("parallel", "parallel"),
        name="merge_out",
    )(a_lat, a_ctx, b, cp, P, P, P, xs, mod, mod, wa, wb, wc, wo)


def _pack_rows(o_ref, y):
    rows, half = y.shape[0], y.shape[1] // 2
    nq = half // LANES
    lo = lax.bitcast_convert_type(y[:, :half].astype(jnp.bfloat16).astype(F32), U32)
    hi = lax.bitcast_convert_type(y[:, half:].astype(jnp.bfloat16).astype(F32), U32)
    w = hi | (lo >> 16)
    for q in range(nq):
        o_ref[pl.ds(q, rows, stride=nq), :] = w[:, q * LANES:(q + 1) * LANES]


def _unpack_rows(ref, q, rows, nq):
    w = ref[pl.ds(q, rows, stride=nq), :]
    lo = lax.bitcast_convert_type(w << 16, F32)
    hi = lax.bitcast_convert_type(w & jnp.uint32(0xFFFF0000), F32)
    return lo, hi


def _router_kernel(x_ref, shb, shc, scb, scc, w_ref, b_ref, h_ref, r_ref, cnt_ref, *, n_lat, tm):
    first = (pl.program_id(0) == 0) & (pl.program_id(1) == 0)

    @pl.when(first)
    def _():
        cnt_ref[...] = jnp.zeros_like(cnt_ref)

    h = _norm_mod(x_ref[0], pl.program_id(1) * tm, n_lat, shb[0], scb[0], shc[0], scc[0])
    _pack_rows(h_ref, h)
    h_hi = h.astype(jnp.bfloat16)
    h_lo = (h - h_hi.astype(F32)).astype(jnp.bfloat16)
    t = jnp.dot(h_hi, w_ref[...], preferred_element_type=F32)
    logits = (t[:, :ROUTE_COLS] + t[:, ROUTE_COLS:]
              + jnp.dot(h_lo, w_ref[:, :ROUTE_COLS], preferred_element_type=F32) + b_ref[...])
    lane = lax.broadcasted_iota(jnp.int32, (tm, ROUTE_COLS), 1)
    ninf = -jnp.inf
    lg = jnp.where(lane < N_GROUPS, logits, ninf)
    mg = jnp.max(lg, axis=-1, keepdims=True)
    g_w = 1.0 / jnp.sum(jnp.exp(lg - mg), axis=-1, keepdims=True)
    g_i = jnp.min(jnp.where(lg == mg, lane, ROUTE_COLS), axis=-1, keepdims=True)
    lo = N_GROUPS + g_i * EXPERTS_PER_GROUP
    le = jnp.where((lane >= lo) & (lane < lo + EXPERTS_PER_GROUP), logits, ninf)
    m1 = jnp.max(le, axis=-1, keepdims=True)
    i1 = jnp.min(jnp.where(le == m1, lane, ROUTE_COLS), axis=-1, keepdims=True)
    le2 = jnp.where(lane == i1, ninf, le)
    m2 = jnp.max(le2, axis=-1, keepdims=True)
    i2 = jnp.min(jnp.where(le2 == m2, lane, ROUTE_COLS), axis=-1, keepdims=True)
    e21 = jnp.exp(m2 - m1)
    w1 = 1.0 / (1.0 + e21)
    w2 = e21 * w1
    onehot = jnp.where((lane == i1) | (lane == i2), 1.0, 0.0)
    ti = lax.broadcasted_iota(jnp.int32, (tm, tm), 0)
    tj = lax.broadcasted_iota(jnp.int32, (tm, tm), 1)
    tri = jnp.where(ti > tj, 1.0, 0.0).astype(jnp.bfloat16)
    before = jnp.dot(tri, onehot.astype(jnp.bfloat16), preferred_element_type=F32) + cnt_ref[...]
    r1 = jnp.sum(jnp.where(lane == i1, before, 0.0), axis=-1, keepdims=True)
    r2 = jnp.sum(jnp.where(lane == i2, before, 0.0), axis=-1, keepdims=True)
    cnt_ref[...] += jnp.sum(onehot, axis=0, keepdims=True)
    out = jnp.where(lane == 0, (i1 - N_GROUPS).astype(F32), 0.0)
    out = jnp.where(lane == 1, (i2 - N_GROUPS).astype(F32), out)
    out = jnp.where(lane == 2, g_w * w1, out)
    out = jnp.where(lane == 3, g_w * w2, out)
    out = jnp.where(lane == 4, r1, out)
    out = jnp.where(lane == 5, r2, out)
    r_ref[0] = out


def _router(xs, mod, w_r, b_r, base, n_lat):
    B, S, D = xs.shape
    tm = _tile(S, 512)
    Q2 = D // (2 * LANES)
    nt = S // tm
    full = lambda shape: pl.BlockSpec(shape, lambda b, i: (0,) * len(shape))
    return pl.pallas_call(
        functools.partial(_router_kernel, n_lat=n_lat, tm=tm),
        out_shape=(jax.ShapeDtypeStruct((B * S * Q2, LANES), U32),
                   jax.ShapeDtypeStruct((B, S, ROUTE_COLS), F32),
                   jax.ShapeDtypeStruct((1, ROUTE_COLS), F32)),
        grid=(B, S // tm),
        in_specs=[pl.BlockSpec((1, tm, D), lambda b, i: (b, i, 0))]
        + _mod_specs(D, base, B, 3) + _mod_specs(D, base, B, 4)
        + [full(w_r.shape), full(b_r.shape)],
        out_specs=(pl.BlockSpec((tm * Q2, LANES), lambda b, i: (b * nt + i, 0)),
                   pl.BlockSpec((1, tm, ROUTE_COLS), lambda b, i: (b, i, 0)),
                   pl.BlockSpec((1, ROUTE_COLS), lambda b, i: (0, 0))),
        compiler_params=_params("arbitrary", "arbitrary"),
        name="moe_router",
    )(xs, mod, mod, mod, mod, w_r, b_r)


def _dispatch_kernel(pos_ref, zlo_ref, h_ref, o_ref, zero_ref, sem, zsem, *, T, td, nq):
    c = pl.program_id(0)

    def zero_fill(e):
        start = pl.multiple_of(zlo_ref[e], nq)
        return pltpu.make_async_copy(zero_ref, o_ref.at[pl.ds(start, MOE_TILE * nq)], zsem)

    @pl.when(c == 0)
    def _():
        zero_ref[...] = jnp.zeros_like(zero_ref)
        for e in range(2 * N_EXPERTS):
            @pl.when(zlo_ref[e] >= 0)
            def _():
                zero_fill(e).start()
        for e in range(2 * N_EXPERTS):
            @pl.when(zlo_ref[e] >= 0)
            def _():
                zero_fill(e).wait()

    def step_bytes():
        return pltpu.make_async_copy(h_ref.at[pl.ds(0, 2 * td * nq)], o_ref.at[pl.ds(0, 2 * td * nq)], sem)

    def issue(r, carry):
        t = c * td + r
        src = h_ref.at[pl.ds(pl.multiple_of(t * nq, nq), nq)]
        for k in range(2):
            dst = o_ref.at[pl.ds(pl.multiple_of(pos_ref[2 * t + k], nq), nq)]
            pltpu.make_async_copy(src, dst, sem).start()
        return carry

    lax.fori_loop(0, td, issue, 0)

    @pl.when(c > 0)
    def _():
        step_bytes().wait()

    @pl.when(c == pl.num_programs(0) - 1)
    def _():
        step_bytes().wait()


def _dispatch(pos, zlo, h2p, T, n_rows):
    nq = h2p.shape[0] // T
    td = _tile(T // 2, 512, SUBLANES)
    return pl.pallas_call(
        functools.partial(_dispatch_kernel, T=T, td=td, nq=nq),
        out_shape=jax.ShapeDtypeStruct((n_rows * nq, LANES), h2p.dtype),
        grid_spec=pltpu.PrefetchScalarGridSpec(
            num_scalar_prefetch=2,
            grid=(T // td,),
            in_specs=[pl.BlockSpec(memory_space=pl.ANY)],
            out_specs=pl.BlockSpec(memory_space=pl.ANY),
            scratch_shapes=[pltpu.VMEM((MOE_TILE * nq, LANES), h2p.dtype),
                            pltpu.SemaphoreType.DMA(()), pltpu.SemaphoreType.DMA(())],
        ),
        compiler_params=pltpu.CompilerParams(dimension_semantics=("arbitrary",),
                                             vmem_limit_bytes=VMEM_LIMIT, has_side_effects=True),
        name="moe_dispatch",
    )(pos, zlo, h2p)


def _expert_kernel(te_ref, nv_ref, x_ref, wg_ref, wu_ref, wd_ref, o_ref, wgu_s, wd_s, *, nq):
    j = pl.program_id(0)
    used = j < nv_ref[0]
    new_expert = (j == 0) | (te_ref[j] != te_ref[jnp.maximum(j - 1, 0)])

    @pl.when(used & new_expert)
    def _():
        wgu_s[:, :EXPERT_HIDDEN] = wg_ref[0, 0].astype(wgu_s.dtype)
        wgu_s[:, EXPERT_HIDDEN:] = wu_ref[0, 0].astype(wgu_s.dtype)
        wd_s[...] = wd_ref[0, 0].astype(wd_s.dtype)

    @pl.when(used)
    def _():
        los, his = [], []
        for q in range(nq):
            lo, hi = _unpack_rows(x_ref, q, MOE_TILE, nq)
            los.append(lo.astype(_MXU))
            his.append(hi.astype(_MXU))
        x = jnp.concatenate(los + his, axis=1)
        hgu = jnp.dot(x, wgu_s[...], preferred_element_type=F32)
        hg = hgu[:, :EXPERT_HIDDEN]
        hid = hg * jax.nn.sigmoid(hg) * hgu[:, EXPERT_HIDDEN:]
        _pack_rows(o_ref, _dot(hid, wd_s[...]))

    @pl.when(jnp.logical_not(used))
    def _():
        o_ref[...] = jnp.zeros_like(o_ref)


def _experts(tile_e, nvalid, xsorted, wg, wu, wd, layer):
    _, E, D, Hd = wg.shape
    nq = D // (2 * LANES)
    NT = xsorted.shape[0] // (MOE_TILE * nq)
    return pl.pallas_call(
        functools.partial(_expert_kernel, nq=nq),
        out_shape=jax.ShapeDtypeStruct(xsorted.shape, U32),
        grid_spec=pltpu.PrefetchScalarGridSpec(
            num_scalar_prefetch=2,
            grid=(NT,),
            in_specs=[pl.BlockSpec((MOE_TILE * nq, LANES), lambda j, te, nv: (j, 0)),
                      pl.BlockSpec((1, 1, D, Hd), lambda j, te, nv: (layer, te[j], 0, 0)),
                      pl.BlockSpec((1, 1, D, Hd), lambda j, te, nv: (layer, te[j], 0, 0)),
                      pl.BlockSpec((1, 1, Hd, D), lambda j, te, nv: (layer, te[j], 0, 0))],
            out_specs=pl.BlockSpec((MOE_TILE * nq, LANES), lambda j, te, nv: (j, 0)),
            scratch_shapes=[pltpu.VMEM((D, 2 * Hd), _MXU), pltpu.VMEM((Hd, D), _MXU)],
        ),
        compiler_params=_params("arbitrary"),
        name="moe_experts",
    )(tile_e, nvalid, xsorted, wg, wu, wd)


def _combine_kernel(pos_ref, y_ref, x_ref, r_ref, g2b, g2c, o_ref, a0, a1, b0, b1, sem_a, sem_b,
                    *, n_lat, tc, S, nq):
    b = pl.program_id(0)
    i = pl.program_id(1)
    nt = pl.num_programs(1)
    step = b * nt + i
    last = pl.num_programs(0) * nt - 1
    sets = ((a0, a1, sem_a), (b0, b1, sem_b))

    def start_gathers(base, bufs):
        buf0, buf1, sem = bufs

        def issue(r, carry):
            dst = pl.ds(pl.multiple_of(r * nq, nq), nq)
            for k, buf in ((0, buf0), (1, buf1)):
                src = y_ref.at[pl.ds(pl.multiple_of(pos_ref[2 * (base + r) + k], nq), nq)]
                pltpu.make_async_copy(src, buf.at[dst], sem).start()
            return carry

        lax.fori_loop(0, tc, issue, 0)

    def consume(bufs):
        buf0, buf1, sem = bufs
        for buf in (buf0, buf1):
            pltpu.make_async_copy(y_ref.at[pl.ds(0, tc * nq)], buf, sem).wait()
        r = r_ref[0]
        c1, c2 = r[:, 2:3], r[:, 3:4]
        rows = i * tc + lax.broadcasted_iota(jnp.int32, (tc, 1), 0)
        lat = rows < n_lat
        half = x_ref.shape[2] // 2
        for q in range(nq):
            lo0, hi0 = _unpack_rows(buf0, q, tc, nq)
            lo1, hi1 = _unpack_rows(buf1, q, tc, nq)
            for col, y0, y1 in ((q * LANES, lo0, lo1), (half + q * LANES, hi0, hi1)):
                cs = slice(col, col + LANES)
                g2 = jnp.where(lat, g2b[0, :, cs], g2c[0, :, cs])
                o_ref[0, :, cs] = x_ref[0, :, cs] + g2 * (c1 * y0 + c2 * y1)

    @pl.when(step == 0)
    def _():
        start_gathers(0, sets[0])

    next_base = jnp.where(i + 1 < nt, b * S + (i + 1) * tc, (b + 1) * S)
    for parity in range(2):
        @pl.when((step < last) & (step % 2 == parity))
        def _():
            start_gathers(next_base, sets[1 - parity])

    for parity in range(2):
        @pl.when(step % 2 == parity)
        def _():
            consume(sets[parity])


def _combine(pos, ysorted, xs, route, mod, base, n_lat, rows):
    B, S, D = xs.shape
    nq = D // (2 * LANES)
    tc = _tile(math.gcd(rows, S), 256, SUBLANES)
    return pl.pallas_call(
        functools.partial(_combine_kernel, n_lat=n_lat, tc=tc, S=S, nq=nq),
        out_shape=jax.ShapeDtypeStruct((B, rows, D), F32),
        grid_spec=pltpu.PrefetchScalarGridSpec(
            num_scalar_prefetch=1,
            grid=(B, rows // tc),
            in_specs=[pl.BlockSpec(memory_space=pl.ANY),
                      pl.BlockSpec((1, tc, D), lambda b, i, pos: (b, i, 0)),
                      pl.BlockSpec((1, tc, ROUTE_COLS), lambda b, i, pos: (b, i, 0))]
            + _mod_specs(D, base, B, 5),
            out_specs=pl.BlockSpec((1, tc, D), lambda b, i, pos: (b, i, 0)),
            scratch_shapes=[pltpu.VMEM((tc * nq, LANES), U32) for _ in range(4)]
            + [pltpu.SemaphoreType.DMA(()), pltpu.SemaphoreType.DMA(())],
        ),
        compiler_params=_params("arbitrary", "arbitrary"),
        name="moe_combine",
    )(pos, ysorted, xs, route, mod, mod)


def _swap_perm():
    j = np.arange(QK_ROPE)
    a, h, f = j // ROPE_AXIS_DIM, (j // ROPE_FREQS) % 2, j % ROPE_FREQS
    return a * ROPE_AXIS_DIM + (1 - h) * ROPE_FREQS + f


def _rope_tables(n_lat, S):
    t = jnp.arange(n_lat)
    row = (t // GRID_W).astype(F32)
    col = (t % GRID_W).astype(F32)
    inv = ROPE_THETA ** (-jnp.arange(0, ROPE_AXIS_DIM, 2, dtype=F32) / ROPE_AXIS_DIM)
    a0, a1 = row[:, None] * inv, col[:, None] * inv
    c64 = jnp.concatenate([jnp.cos(a0), jnp.cos(a0), jnp.cos(a1), jnp.cos(a1)], axis=1)
    s64 = jnp.concatenate([-jnp.sin(a0), jnp.sin(a0), -jnp.sin(a1), jnp.sin(a1)], axis=1)
    nc = S - n_lat
    c64 = jnp.concatenate([c64, jnp.ones((nc, QK_ROPE), F32)], axis=0)
    s64 = jnp.concatenate([s64, jnp.zeros((nc, QK_ROPE), F32)], axis=0)
    ct = jnp.concatenate([jnp.ones((S, QK_NOPE), F32), c64, jnp.zeros((S, QK_ROPE), F32)], axis=1)
    st = jnp.concatenate([jnp.zeros((S, QK_NOPE), F32), s64, jnp.zeros((S, QK_ROPE), F32)], axis=1)
    return ct, st


def _head_gains(g, perm):
    z = jnp.zeros((QK_ROPE,), F32)
    ga = jnp.concatenate([g[:QK_NOPE], g[QK_NOPE:], z])
    gb = jnp.concatenate([jnp.zeros((QK_NOPE,), F32), g[QK_NOPE:][perm], z])
    return jnp.stack([ga, gb], axis=0)


def _prep_w_in(w, D, perm):
    o = np.cumsum((0, Q_LORA, KV_LORA, QK_ROPE, 2 * SGU_WIDTH, POOL_WIDTH, 3 * D))
    wq, wkv, wkr, wsgu, wpool, wgate = (w[:, o[k]:o[k + 1]] for k in range(6))
    pad = jnp.zeros((D, MLA_BLOCK - Q_LORA - KV_LORA - 2 * QK_ROPE), w.dtype)
    return jnp.concatenate([wsgu, wpool, wq, wkv, pad, wkr, wkr[:, perm], wgate], axis=1).astype(_MXU)


def _slots(route, counts, T, nq):
    e = route[:, 0:2].astype(jnp.int32)
    rank = route[:, 4:6].astype(jnp.int32)
    cnt = counts[0, N_GROUPS:N_GROUPS + N_EXPERTS].astype(jnp.int32)
    padded = (cnt + MOE_TILE - 1) // MOE_TILE * MOE_TILE
    end = jnp.cumsum(padded)
    off = end - padded
    own = e[:, :, None] == jnp.arange(N_EXPERTS, dtype=jnp.int32)
    pos = rank + jnp.sum(jnp.where(own, off, 0), axis=-1)
    n_rows = 2 * T + N_EXPERTS * MOE_TILE
    NT = n_rows // MOE_TILE
    nvalid = (end[-1] // MOE_TILE).astype(jnp.int32)
    tiles = jnp.minimum(jnp.arange(NT, dtype=jnp.int32), nvalid - 1)
    tile_e = jnp.sum((end[None, :] <= (tiles * MOE_TILE)[:, None]).astype(jnp.int32), axis=1)
    tail = nvalid + jnp.arange(N_EXPERTS, dtype=jnp.int32)
    zlo = jnp.concatenate([jnp.where(cnt > 0, (end - MOE_TILE) * nq, -1),
                           jnp.where(tail < NT, tail * (MOE_TILE * nq), -1)]).astype(jnp.int32)
    return (pos.reshape(-1) * nq).astype(jnp.int32), zlo, tile_e, nvalid.reshape(1), n_rows


def kernel(x, c, ctx, c_ctx, w_ada, b_ada, w_in, g_cq, g_ckv, w_uq, w_ukv, g_q, g_k, w_sgu, b_sgu, g_sgu,
           w_pool, s_pool, w_ao, w_bo, w_co, w_out, w_rg, b_rg, w_re, b_re, w_e_gate, w_e_up, w_e_down):
    B, n, D = x.shape
    nc = ctx.shape[1]
    S = n + nc
    T = B * S
    L = w_ada.shape[0]
    perm = _swap_perm()

    R = -(-(B + 1) // SUBLANES) * SUBLANES
    cc = jnp.concatenate([c, c_ctx[None], jnp.zeros((R - B - 1, D), F32)], axis=0)
    mod = _ada(cc, w_ada, b_ada)
    mod2 = mod.reshape(L * R, 1, 6 * D)

    ct, st = _rope_tables(n, S)
    xs = jnp.concatenate([x, ctx], axis=1)

    for i in range(L):
        base = i * R
        w_in_r = _prep_w_in(w_in[i], D, perm)
        wq = w_uq[i].reshape(Q_LORA, MLA_HEADS, QK_HEAD)
        wq = jnp.concatenate([wq, wq[:, :, QK_NOPE:][:, :, perm]], axis=2).reshape(Q_LORA, -1).astype(_MXU)
        wkv = w_ukv[i].astype(_MXU)
        bias = jnp.repeat(b_sgu[i].T, SGU_GROUP_CH, axis=1)
        w_r = jnp.concatenate([w_rg[i], jnp.moveaxis(w_re[i], 0, 1).reshape(D, N_EXPERTS),
                               jnp.zeros((D, ROUTE_COLS - N_GROUPS - N_EXPERTS), F32)], axis=1)
        b_r = jnp.concatenate([b_rg[i], b_re[i].reshape(-1),
                               jnp.zeros((ROUTE_COLS - N_GROUPS - N_EXPERTS,), F32)])[None]
        w_r_hi = w_r.astype(jnp.bfloat16)
        w_r = jnp.concatenate([w_r_hi, (w_r - w_r_hi.astype(F32)).astype(jnp.bfloat16)], axis=1)

        P = _inproj(xs, mod2, w_in_r, base, n)
        q, k, v = _mla_front(P, wq, wkv, g_cq[i][None], g_ckv[i][None],
                             _head_gains(g_q[i], perm), _head_gains(g_k[i], perm), ct, st)
        a_lat, a_ctx = _attention(q, k, v, n)
        bb = _sgu(P, g_sgu[i][None], w_sgu[i].astype(_MXU), bias)
        cp = _pool(P, w_pool[i].astype(_MXU), s_pool[i][None], n)
        xs = _merge(a_lat, a_ctx, bb, cp, P, xs, mod2, w_ao[i].astype(_MXU), w_bo[i].astype(_MXU),
                    w_co[i].astype(_MXU), w_out[i].astype(_MXU), base, n)

        h2, route, counts = _router(xs, mod2, w_r, b_r, base, n)
        pos, zlo, tile_e, nvalid, n_rows = _slots(route.reshape(T, ROUTE_COLS), counts, T, D // (2 * LANES))
        xsorted = _dispatch(pos, zlo, h2, T, n_rows)
        ysorted = _experts(tile_e, nvalid, xsorted, w_e_gate, w_e_up, w_e_down, i)
        xs = _combine(pos, ysorted, xs, route, mod2, base, n, S if i < L - 1 else n)

    return xs
```

```python
import functools
import math

import jax
import jax.numpy as jnp
import numpy as np
from jax import lax
from jax.experimental import pallas as pl
from jax.experimental.pallas import tpu as pltpu

GRID_W = 64
EPS = 1e-6
MLA_HEADS = 8
QK_NOPE = 128
QK_ROPE = 64
V_HEAD = 128
QK_HEAD = QK_NOPE + QK_ROPE
Q_LORA = 512
KV_LORA = 256
MLA_WIDTH = MLA_HEADS * V_HEAD
ROPE_THETA = 10000.0
ROPE_AXIS_DIM = QK_ROPE // 2
ROPE_FREQS = ROPE_AXIS_DIM // 2
ATTN_SCALE = 1.0 / math.sqrt(QK_HEAD)
LOG2E = math.log2(math.e)
SGU_GROUPS = 8
SGU_CHUNK = 128
SGU_GROUP_CH = 128
SGU_WIDTH = SGU_GROUPS * SGU_GROUP_CH
POOL_WINDOWS = (2, 4, 8, 16)
POOL_GROUP_CH = 256
POOL_WIDTH = len(POOL_WINDOWS) * POOL_GROUP_CH
N_GROUPS = 4
EXPERTS_PER_GROUP = 8
N_EXPERTS = N_GROUPS * EXPERTS_PER_GROUP
EXPERT_HIDDEN = 512

LANES = 128
SUBLANES = 8
HEAD_PAD = 256
VMEM_LIMIT = 56 * 1024 * 1024

MLA_BLOCK = 1024
COL_SGU = 0
COL_POOL = 2 * SGU_WIDTH
COL_MLA = COL_POOL + POOL_WIDTH
COL_GATE = COL_MLA + MLA_BLOCK

POOL_HALO = 16
ROUTE_COLS = LANES
MOE_TILE = 512
NORM_CHUNK = 32

_MXU = jnp.bfloat16
_ACT = jnp.bfloat16
F32 = jnp.float32
U32 = jnp.uint32


def _params(*sem):
    return pltpu.CompilerParams(dimension_semantics=sem, vmem_limit_bytes=VMEM_LIMIT)


def _tile(total, cap, mult=LANES):
    best = None
    for t in range(mult, min(total, cap) + 1, mult):
        if total % t == 0:
            best = t
    assert best is not None, (total, cap, mult)
    return best


def _dot(a, b):
    return jnp.dot(a.astype(_MXU), b.astype(_MXU), preferred_element_type=F32)


def _ada_kernel(c_ref, w_ref, b_ref, o_ref):
    c = c_ref[...]
    s = c * jax.nn.sigmoid(c)
    o_ref[0] = _dot(s, w_ref[0]) + b_ref[0]


def _ada(cc, w_ada, b_ada):
    L, D, N = w_ada.shape
    R = cc.shape[0]
    tn = _tile(N, 1024)
    return pl.pallas_call(
        _ada_kernel,
        out_shape=jax.ShapeDtypeStruct((L, R, N), F32),
        grid=(L, N // tn),
        in_specs=[
            pl.BlockSpec((R, D), lambda l, j: (0, 0)),
            pl.BlockSpec((1, D, tn), lambda l, j: (l, 0, j)),
            pl.BlockSpec((1, 1, tn), lambda l, j: (l, 0, j)),
        ],
        out_specs=pl.BlockSpec((1, R, tn), lambda l, j: (l, 0, j)),
        compiler_params=_params("parallel", "parallel"),
        name="ada_mod",
    )(cc, w_ada, b_ada.reshape(L, 1, N))


def _norm_mod(x, row0, n_lat, sh_b, sc_b, sh_c, sc_c):
    ms = jnp.mean(x * x, axis=-1, keepdims=True)
    xn = x * lax.rsqrt(ms + EPS)
    rows = row0 + lax.broadcasted_iota(jnp.int32, (x.shape[0], 1), 0)
    lat = rows < n_lat
    sc = jnp.where(lat, sc_b, sc_c)
    sh = jnp.where(lat, sh_b, sh_c)
    return xn * (1.0 + sc) + sh


def _mod_specs(D, base, B, col):
    return [pl.BlockSpec((1, 1, D), lambda *g: (base + g[0], 0, col)),
            pl.BlockSpec((1, 1, D), lambda *g: (base + B, 0, col))]


def _inproj_kernel(x_ref, shb, shc, scb, scc, w_ref, o_ref, h_ref, *, n_lat, tm):
    @pl.when(pl.program_id(2) == 0)
    def _():
        def chunk(c, carry):
            r0 = pl.multiple_of(c * NORM_CHUNK, NORM_CHUNK)
            h = _norm_mod(x_ref[0, pl.ds(r0, NORM_CHUNK), :], pl.program_id(1) * tm + r0, n_lat,
                          shb[0], scb[0], shc[0], scc[0])
            h_ref[pl.ds(r0, NORM_CHUNK), :] = h.astype(h_ref.dtype)
            return carry
        lax.fori_loop(0, tm // NORM_CHUNK, chunk, 0)
    o_ref[0] = jnp.dot(h_ref[...], w_ref[...], preferred_element_type=F32).astype(o_ref.dtype)


def _inproj(xs, mod, w, base, n_lat):
    B, S, D = xs.shape
    NP = w.shape[1]
    tm = _tile(S, 1024)
    tn = _tile(NP, 2048)
    return pl.pallas_call(
        functools.partial(_inproj_kernel, n_lat=n_lat, tm=tm),
        out_shape=jax.ShapeDtypeStruct((B, S, NP), _ACT),
        grid=(B, S // tm, NP // tn),
        in_specs=[pl.BlockSpec((1, tm, D), lambda b, i, j: (b, i, 0))]
        + _mod_specs(D, base, B, 0) + _mod_specs(D, base, B, 1)
        + [pl.BlockSpec((D, tn), lambda b, i, j: (0, j))],
        out_specs=pl.BlockSpec((1, tm, tn), lambda b, i, j: (b, i, j)),
        scratch_shapes=[pltpu.VMEM((tm, D), _MXU)],
        compiler_params=_params("parallel", "parallel", "arbitrary"),
        name="in_proj",
    )(xs, mod, mod, mod, mod, w)


def _rope_half(b, gct, gst):
    return b * gct + pltpu.roll(b, QK_ROPE, axis=1) * gst


def _sumsq_rope(b):
    lane = lax.broadcasted_iota(jnp.int32, (1, LANES), 1)
    return jnp.sum(jnp.where(lane < QK_ROPE, b * b, 0.0), axis=-1, keepdims=True)


def _mla_kernel(p_ref, wq_ref, wkv_ref, gcq_ref, gckv_ref, gq_ref, gk_ref, ct_ref, st_ref,
                q_ref, k_ref, v_ref):
    p = p_ref[0].astype(F32)
    pq = p[:, :Q_LORA]
    pkv = p[:, Q_LORA:Q_LORA + KV_LORA]
    kr = p[:, MLA_BLOCK - LANES:]
    cq = pq * lax.rsqrt(jnp.mean(pq * pq, axis=-1, keepdims=True) + EPS) * gcq_ref[...]
    ckv = pkv * lax.rsqrt(jnp.mean(pkv * pkv, axis=-1, keepdims=True) + EPS) * gckv_ref[...]
    qf = _dot(cq, wq_ref[...])
    kvf = _dot(ckv, wkv_ref[...])
    ct = ct_ref[...]
    st = st_ref[...]
    gq = gq_ref[...]
    gk = gk_ref[...]
    gq_n, q_gct, q_gst = gq[0:1, :QK_NOPE], gq[0:1, QK_NOPE:] * ct, gq[1:2, QK_NOPE:] * st
    gk_n = gk[0:1, :QK_NOPE]
    k_rot = _rope_half(kr, gk[0:1, QK_NOPE:] * ct, gk[1:2, QK_NOPE:] * st)
    kr_ss = _sumsq_rope(kr)
    for h in range(MLA_HEADS):
        lo = h * HEAD_PAD
        qa, qb = qf[:, lo:lo + QK_NOPE], qf[:, lo + QK_NOPE:lo + HEAD_PAD]
        rq = lax.rsqrt((jnp.sum(qa * qa, axis=-1, keepdims=True) + _sumsq_rope(qb)) * (1.0 / QK_HEAD) + EPS)
        rq = rq * (ATTN_SCALE * LOG2E)
        q_ref[0, :, lo:lo + QK_NOPE] = (qa * gq_n * rq).astype(q_ref.dtype)
        q_ref[0, :, lo + QK_NOPE:lo + HEAD_PAD] = (_rope_half(qb, q_gct, q_gst) * rq).astype(q_ref.dtype)
        ka = kvf[:, lo:lo + QK_NOPE]
        rk = lax.rsqrt((jnp.sum(ka * ka, axis=-1, keepdims=True) + kr_ss) * (1.0 / QK_HEAD) + EPS)
        k_ref[0, :, lo:lo + QK_NOPE] = (ka * gk_n * rk).astype(k_ref.dtype)
        k_ref[0, :, lo + QK_NOPE:lo + HEAD_PAD] = (k_rot * rk).astype(k_ref.dtype)
        v_ref[0, :, h * V_HEAD:(h + 1) * V_HEAD] = kvf[:, lo + QK_NOPE:lo + HEAD_PAD].astype(v_ref.dtype)


def _mla_front(P, wq, wkv, gcq, gckv, gq2, gk2, ct, st):
    B, S, _ = P.shape
    tm = _tile(S, 512)
    HW = MLA_HEADS * HEAD_PAD
    cblk = COL_MLA // MLA_BLOCK
    full = lambda shape: pl.BlockSpec(shape, lambda b, i: (0,) * len(shape))
    return pl.pallas_call(
        _mla_kernel,
        out_shape=(jax.ShapeDtypeStruct((B, S, HW), _ACT),
                   jax.ShapeDtypeStruct((B, S, HW), _ACT),
                   jax.ShapeDtypeStruct((B, S, MLA_WIDTH), _ACT)),
        grid=(B, S // tm),
        in_specs=[pl.BlockSpec((1, tm, MLA_BLOCK), lambda b, i: (b, i, cblk)),
                  full(wq.shape), full(wkv.shape), full(gcq.shape), full(gckv.shape),
                  full(gq2.shape), full(gk2.shape),
                  pl.BlockSpec((tm, LANES), lambda b, i: (i, QK_NOPE // LANES)),
                  pl.BlockSpec((tm, LANES), lambda b, i: (i, QK_NOPE // LANES))],
        out_specs=(pl.BlockSpec((1, tm, HW), lambda b, i: (b, i, 0)),
                   pl.BlockSpec((1, tm, HW), lambda b, i: (b, i, 0)),
                   pl.BlockSpec((1, tm, MLA_WIDTH), lambda b, i: (b, i, 0))),
        compiler_params=_params("parallel", "parallel"),
        name="mla_front",
    )(P, wq, wkv, gcq, gckv, gq2, gk2, ct, st)


def _attn_kernel(q_ref, k_ref, v_ref, o_ref):
    nk = k_ref.shape[1]
    ones = (lax.broadcasted_iota(jnp.int32, (nk, V_HEAD), 1) == 0).astype(v_ref.dtype)
    for h in range(MLA_HEADS):
        q = q_ref[0, :, h * HEAD_PAD:(h + 1) * HEAD_PAD]
        k = k_ref[0, :, h * HEAD_PAD:(h + 1) * HEAD_PAD]
        v = jnp.concatenate([v_ref[0, :, h * V_HEAD:(h + 1) * V_HEAD], ones], axis=1)
        s = lax.dot_general(q, k, (((1,), (1,)), ((), ())), preferred_element_type=F32)
        p = jnp.exp2(s - jnp.max(s, axis=-1, keepdims=True))
        o = jnp.dot(p.astype(v.dtype), v, preferred_element_type=F32)
        o_ref[0, :, h * V_HEAD:(h + 1) * V_HEAD] = (
            o[:, :V_HEAD] / o[:, V_HEAD:V_HEAD + 1]).astype(o_ref.dtype)


def _attention(q, k, v, n_lat):
    B, S, HW = q.shape
    nc = S - n_lat
    assert n_lat % nc == 0, (n_lat, nc)
    tq = _tile(n_lat, 512)
    a_lat = pl.pallas_call(
        _attn_kernel,
        out_shape=jax.ShapeDtypeStruct((B, n_lat, MLA_WIDTH), _ACT),
        grid=(B, n_lat // tq),
        in_specs=[pl.BlockSpec((1, tq, HW), lambda b, i: (b, i, 0)),
                  pl.BlockSpec((1, S, HW), lambda b, i: (b, 0, 0)),
                  pl.BlockSpec((1, S, MLA_WIDTH), lambda b, i: (b, 0, 0))],
        out_specs=pl.BlockSpec((1, tq, MLA_WIDTH), lambda b, i: (b, i, 0)),
        compiler_params=_params("parallel", "arbitrary"),
        name="attention",
    )(q, k, v)
    cblk = n_lat // nc
    ctx_rows = lambda w: pl.BlockSpec((1, nc, w), lambda b: (b, cblk, 0))
    a_ctx = pl.pallas_call(
        _attn_kernel,
        out_shape=jax.ShapeDtypeStruct((B, nc, MLA_WIDTH), _ACT),
        grid=(B,),
        in_specs=[ctx_rows(HW), ctx_rows(HW), ctx_rows(MLA_WIDTH)],
        out_specs=pl.BlockSpec((1, nc, MLA_WIDTH), lambda b: (b, 0, 0)),
        compiler_params=_params("parallel"),
        name="attention_ctx",
    )(q, k, v)
    return a_lat, a_ctx


def _gelu_tanh(x):
    return 0.5 * x * (1.0 + jnp.tanh(math.sqrt(2.0 / math.pi) * (x + 0.044715 * (x * x * x))))


def _sgu_kernel(p_ref, g_ref, w_ref, b_ref, o_ref, *, tm):
    gp = _gelu_tanh(p_ref[0].astype(F32))
    u = gp[:, :SGU_WIDTH]
    v = gp[:, SGU_WIDTH:]
    v = v * lax.rsqrt(jnp.mean(v * v, axis=-1, keepdims=True) + EPS) * g_ref[...]
    vb = v.astype(_MXU)
    bias = b_ref[...]
    for c in range(tm // SGU_CHUNK):
        rs = slice(c * SGU_CHUNK, (c + 1) * SGU_CHUNK)
        for g in range(SGU_GROUPS):
            cs = slice(g * SGU_GROUP_CH, (g + 1) * SGU_GROUP_CH)
            mix = jnp.dot(w_ref[g], vb[rs, cs], preferred_element_type=F32) + bias[:, cs]
            o_ref[0, rs, cs] = (u[rs, cs] * mix).astype(o_ref.dtype)


def _sgu(P, g_sgu, w_sgu, bias):
    B, S, _ = P.shape
    tm = _tile(S, 512)
    full = lambda shape: pl.BlockSpec(shape, lambda b, i: (0,) * len(shape))
    return pl.pallas_call(
        functools.partial(_sgu_kernel, tm=tm),
        out_shape=jax.ShapeDtypeStruct((B, S, SGU_WIDTH), _ACT),
        grid=(B, S // tm),
        in_specs=[pl.BlockSpec((1, tm, 2 * SGU_WIDTH), lambda b, i: (b, i, COL_SGU // (2 * SGU_WIDTH))),
                  full(g_sgu.shape), full(w_sgu.shape), full(bias.shape)],
        out_specs=pl.BlockSpec((1, tm, SGU_WIDTH), lambda b, i: (b, i, 0)),
        compiler_params=_params("parallel", "parallel"),
        name="spatial_gating",
    )(P, g_sgu, w_sgu, bias)


def _window_sum(x, g):
    n, C = x.shape
    z = jnp.zeros((POOL_HALO, C), F32)
    xe = jnp.concatenate([z, x, z], axis=0)
    ne = n + 2 * POOL_HALO
    w = xe + pltpu.roll(xe, 1, axis=0)
    for s in range(g):
        k = 1 << s
        w = pltpu.roll(w, k, axis=0) + pltpu.roll(w, ne - k, axis=0)
    return w[POOL_HALO:POOL_HALO + n]


def _pool_kernel(p_ref, w_ref, s_ref, o_ref, *, n_lat):
    S = p_ref.shape[1]
    for lo, n in ((0, n_lat), (n_lat, S - n_lat)):
        t = lax.broadcasted_iota(jnp.int32, (n, 1), 0)
        for g, win in enumerate(POOL_WINDOWS):
            cs = slice(g * POOL_GROUP_CH, (g + 1) * POOL_GROUP_CH)
            x = p_ref[0, lo:lo + n, cs].astype(F32)
            half = win // 2
            cnt = (jnp.minimum(t + half, n) - jnp.maximum(t - half, 0)).astype(F32)
            m = _window_sum(x, g) / cnt - x
            y = _dot(m, w_ref[g]) * s_ref[:, cs]
            o_ref[0, lo:lo + n, cs] = y.astype(o_ref.dtype)


def _pool(P, w_pool, s_pool, n_lat):
    B, S, _ = P.shape
    full = lambda shape: pl.BlockSpec(shape, lambda b: (0,) * len(shape))
    return pl.pallas_call(
        functools.partial(_pool_kernel, n_lat=n_lat),
        out_shape=jax.ShapeDtypeStruct((B, S, POOL_WIDTH), _ACT),
        grid=(B,),
        in_specs=[pl.BlockSpec((1, S, POOL_WIDTH), lambda b: (b, 0, COL_POOL // POOL_WIDTH)),
                  full(w_pool.shape), full(s_pool.shape)],
        out_specs=pl.BlockSpec((1, S, POOL_WIDTH), lambda b: (b, 0, 0)),
        compiler_params=_params("parallel"),
        name="multiscale_pool",
    )(P, w_pool, s_pool)


def _merge_kernel(al_ref, ac_ref, b_ref, c_ref, ga_ref, gb_ref, gc_ref, x_ref, g1b, g1c,
                  wa_ref, wb_ref, wc_ref, wo_ref, o_ref, *, n_lat, tm):
    def gate(ref):
        return jax.nn.sigmoid(ref[0].astype(F32))
    a = jnp.where(pl.program_id(1) * tm < n_lat, al_ref[0], ac_ref[0])
    y = gate(ga_ref) * jnp.dot(a, wa_ref[...], preferred_element_type=F32)
    y = y + gate(gb_ref) * jnp.dot(b_ref[0], wb_ref[...], preferred_element_type=F32)
    y = y + gate(gc_ref) * jnp.dot(c_ref[0], wc_ref[...], preferred_element_type=F32)
    out = _dot(y, wo_ref[...])
    rows = pl.program_id(1) * tm + lax.broadcasted_iota(jnp.int32, (tm, 1), 0)
    g1 = jnp.where(rows < n_lat, g1b[0], g1c[0])
    o_ref[0] = x_ref[0] + g1 * out


def _merge(a_lat, a_ctx, b, cp, P, xs, mod, wa, wb, wc, wo, base, n_lat):
    B, S, D = xs.shape
    tm = _tile(math.gcd(n_lat, S - n_lat), 256)
    gblk = COL_GATE // D
    nl = n_lat // tm
    tok = lambda w: pl.BlockSpec((1, tm, w), lambda bb, i: (bb, i, 0))
    att_lat = pl.BlockSpec((1, tm, MLA_WIDTH), lambda bb, i: (bb, jnp.minimum(i, nl - 1), 0))
    att_ctx = pl.BlockSpec((1, tm, MLA_WIDTH), lambda bb, i: (bb, jnp.maximum(i - nl, 0), 0))
    gate = lambda k: pl.BlockSpec((1, tm, D), lambda bb, i: (bb, i, gblk + k))
    wres = lambda shape: pl.BlockSpec(shape, lambda bb, i: (0, 0), pipeline_mode=pl.Buffered(1))
    return pl.pallas_call(
        functools.partial(_merge_kernel, n_lat=n_lat, tm=tm),
        out_shape=jax.ShapeDtypeStruct((B, S, D), F32),
        grid=(B, S // tm),
        in_specs=[att_lat, att_ctx, tok(SGU_WIDTH), tok(POOL_WIDTH), gate(0), gate(1), gate(2), tok(D)]
        + _mod_specs(D, base, B, 2)
        + [wres(wa.shape), wres(wb.shape), wres(wc.shape), wres(wo.shape)],
        out_specs=tok(D),
        compiler_params=_params("parallel", "parallel"),
        name="merge_out",
    )(a_lat, a_ctx, b, cp, P, P, P, xs, mod, mod, wa, wb, wc, wo)


def _pack_rows(o_ref, y):
    rows, half = y.shape[0], y.shape[1] // 2
    nq = half // LANES
    lo = lax.bitcast_convert_type(y[:, :half].astype(jnp.bfloat16).astype(F32), U32)
    hi = lax.bitcast_convert_type(y[:, half:].astype(jnp.bfloat16).astype(F32), U32)
    w = hi | (lo >> 16)
    for q in range(nq):
        o_ref[pl.ds(q, rows, stride=nq), :] = w[:, q * LANES:(q + 1) * LANES]


def _unpack_rows(ref, q, rows, nq):
    w = ref[pl.ds(q, rows, stride=nq), :]
    lo = lax.bitcast_convert_type(w << 16, F32)
    hi = lax.bitcast_convert_type(w & jnp.uint32(0xFFFF0000), F32)
    return lo, hi


def _router_kernel(x_ref, shb, shc, scb, scc, w_ref, b_ref, h_ref, r_ref, cnt_ref, *, n_lat, tm):
    first = (pl.program_id(0) == 0) & (pl.program_id(1) == 0)

    @pl.when(first)
    def _():
        cnt_ref[...] = jnp.zeros_like(cnt_ref)

    h = _norm_mod(x_ref[0], pl.program_id(1) * tm, n_lat, shb[0], scb[0], shc[0], scc[0])
    _pack_rows(h_ref, h)
    h_hi = h.astype(jnp.bfloat16)
    h_lo = (h - h_hi.astype(F32)).astype(jnp.bfloat16)
    t = jnp.dot(h_hi, w_ref[...], preferred_element_type=F32)
    logits = (t[:, :ROUTE_COLS] + t[:, ROUTE_COLS:]
              + jnp.dot(h_lo, w_ref[:, :ROUTE_COLS], preferred_element_type=F32) + b_ref[...])
    lane = lax.broadcasted_iota(jnp.int32, (tm, ROUTE_COLS), 1)
    ninf = -jnp.inf
    lg = jnp.where(lane < N_GROUPS, logits, ninf)
    mg = jnp.max(lg, axis=-1, keepdims=True)
    g_w = 1.0 / jnp.sum(jnp.exp(lg - mg), axis=-1, keepdims=True)
    g_i = jnp.min(jnp.where(lg == mg, lane, ROUTE_COLS), axis=-1, keepdims=True)
    lo = N_GROUPS + g_i * EXPERTS_PER_GROUP
    le = jnp.where((lane >= lo) & (lane < lo + EXPERTS_PER_GROUP), logits, ninf)
    m1 = jnp.max(le, axis=-1, keepdims=True)
    i1 = jnp.min(jnp.where(le == m1, lane, ROUTE_COLS), axis=-1, keepdims=True)
    le2 = jnp.where(lane == i1, ninf, le)
    m2 = jnp.max(le2, axis=-1, keepdims=True)
    i2 = jnp.min(jnp.where(le2 == m2, lane, ROUTE_COLS), axis=-1, keepdims=True)
    e21 = jnp.exp(m2 - m1)
    w1 = 1.0 / (1.0 + e21)
    w2 = e21 * w1
    onehot = jnp.where((lane == i1) | (lane == i2), 1.0, 0.0)
    ti = lax.broadcasted_iota(jnp.int32, (tm, tm), 0)
    tj = lax.broadcasted_iota(jnp.int32, (tm, tm), 1)
    tri = jnp.where(ti > tj, 1.0, 0.0).astype(jnp.bfloat16)
    before = jnp.dot(tri, onehot.astype(jnp.bfloat16), preferred_element_type=F32) + cnt_ref[...]
    r1 = jnp.sum(jnp.where(lane == i1, before, 0.0), axis=-1, keepdims=True)
    r2 = jnp.sum(jnp.where(lane == i2, before, 0.0), axis=-1, keepdims=True)
    cnt_ref[...] += jnp.sum(onehot, axis=0, keepdims=True)
    out = jnp.where(lane == 0, (i1 - N_GROUPS).astype(F32), 0.0)
    out = jnp.where(lane == 1, (i2 - N_GROUPS).astype(F32), out)
    out = jnp.where(lane == 2, g_w * w1, out)
    out = jnp.where(lane == 3, g_w * w2, out)
    out = jnp.where(lane == 4, r1, out)
    out = jnp.where(lane == 5, r2, out)
    r_ref[0] = out


def _router(xs, mod, w_r, b_r, base, n_lat):
    B, S, D = xs.shape
    tm = _tile(S, 512)
    Q2 = D // (2 * LANES)
    nt = S // tm
    full = lambda shape: pl.BlockSpec(shape, lambda b, i: (0,) * len(shape))
    return pl.pallas_call(
        functools.partial(_router_kernel, n_lat=n_lat, tm=tm),
        out_shape=(jax.ShapeDtypeStruct((B * S * Q2, LANES), U32),
                   jax.ShapeDtypeStruct((B, S, ROUTE_COLS), F32),
                   jax.ShapeDtypeStruct((1, ROUTE_COLS), F32)),
        grid=(B, S // tm),
        in_specs=[pl.BlockSpec((1, tm, D), lambda b, i: (b, i, 0))]
        + _mod_specs(D, base, B, 3) + _mod_specs(D, base, B, 4)
        + [full(w_r.shape), full(b_r.shape)],
        out_specs=(pl.BlockSpec((tm * Q2, LANES), lambda b, i: (b * nt + i, 0)),
                   pl.BlockSpec((1, tm, ROUTE_COLS), lambda b, i: (b, i, 0)),
                   pl.BlockSpec((1, ROUTE_COLS), lambda b, i: (0, 0))),
        compiler_params=_params("arbitrary", "arbitrary"),
        name="moe_router",
    )(xs, mod, mod, mod, mod, w_r, b_r)


def _dispatch_kernel(pos_ref, zlo_ref, h_ref, o_ref, zero_ref, stage, ld_sem, sc_sem, zsem, *, td, nq):
    c = pl.program_id(0)
    n = pl.num_programs(0)
    rows = td * nq
    slot = c % DISPATCH_SLOTS

    def load(chunk, s):
        src = h_ref.at[pl.ds(pl.multiple_of(chunk * rows, rows), rows)]
        return pltpu.make_async_copy(src, stage.at[s], ld_sem.at[s])

    def row_copies_of(s):
        return pltpu.make_async_copy(h_ref.at[pl.ds(0, 2 * rows)], o_ref.at[pl.ds(0, 2 * rows)], sc_sem.at[s])

    def zero_fill(e):
        start = pl.multiple_of(zlo_ref[e], nq)
        return pltpu.make_async_copy(zero_ref, o_ref.at[pl.ds(start, MOE_TILE * nq)], zsem)

    @pl.when(c == 0)
    def _():
        zero_ref[...] = jnp.zeros_like(zero_ref)
        for e in range(2 * N_EXPERTS):
            @pl.when(zlo_ref[e] >= 0)
            def _():
                zero_fill(e).start()
        for e in range(2 * N_EXPERTS):
            @pl.when(zlo_ref[e] >= 0)
            def _():
                zero_fill(e).wait()

        load(0, 0).start()

    @pl.when(c >= 2)
    def _():
        row_copies_of((c + 1) % DISPATCH_SLOTS).wait()

    @pl.when(c + 1 < n)
    def _():
        load(c + 1, (c + 1) % DISPATCH_SLOTS).start()

    load(c, slot).wait()

    def issue(r, carry):
        src = stage.at[slot, pl.ds(pl.multiple_of(r * nq, nq), nq)]
        for k in range(2):
            dst = o_ref.at[pl.ds(pl.multiple_of(pos_ref[2 * (c * td + r) + k], nq), nq)]
            pltpu.make_async_copy(src, dst, sc_sem.at[slot]).start()
        return carry

    lax.fori_loop(0, td, issue, 0)

    @pl.when(c == n - 1)
    def _():
        @pl.when(c >= 1)
        def _():
            row_copies_of((c + DISPATCH_SLOTS - 1) % DISPATCH_SLOTS).wait()
        row_copies_of(slot).wait()


DISPATCH_SLOTS = 3


def _dispatch(pos, zlo, h2p, T, n_rows):
    nq = h2p.shape[0] // T
    td = _tile(T // 2, 512, SUBLANES)
    return pl.pallas_call(
        functools.partial(_dispatch_kernel, td=td, nq=nq),
        out_shape=jax.ShapeDtypeStruct((n_rows * nq, LANES), h2p.dtype),
        grid_spec=pltpu.PrefetchScalarGridSpec(
            num_scalar_prefetch=2,
            grid=(T // td,),
            in_specs=[pl.BlockSpec(memory_space=pl.ANY)],
            out_specs=pl.BlockSpec(memory_space=pl.ANY),
            scratch_shapes=[pltpu.VMEM((MOE_TILE * nq, LANES), h2p.dtype),
                            pltpu.VMEM((DISPATCH_SLOTS, td * nq, LANES), h2p.dtype),
                            pltpu.SemaphoreType.DMA((DISPATCH_SLOTS,)),
                            pltpu.SemaphoreType.DMA((DISPATCH_SLOTS,)),
                            pltpu.SemaphoreType.DMA(())],
        ),
        compiler_params=pltpu.CompilerParams(dimension_semantics=("arbitrary",),
                                             vmem_limit_bytes=VMEM_LIMIT, has_side_effects=True),
        name="moe_dispatch",
    )(pos, zlo, h2p)


def _expert_kernel(te_ref, nv_ref, x_ref, wg_ref, wu_ref, wd_ref, o_ref, wgu_s, wd_s, *, nq):
    j = pl.program_id(0)
    used = j < nv_ref[0]
    new_expert = (j == 0) | (te_ref[j] != te_ref[jnp.maximum(j - 1, 0)])

    @pl.when(used & new_expert)
    def _():
        wgu_s[:, :EXPERT_HIDDEN] = wg_ref[0, 0].astype(wgu_s.dtype)
        wgu_s[:, EXPERT_HIDDEN:] = wu_ref[0, 0].astype(wgu_s.dtype)
        wd_s[...] = wd_ref[0, 0].astype(wd_s.dtype)

    @pl.when(used)
    def _():
        los, his = [], []
        for q in range(nq):
            lo, hi = _unpack_rows(x_ref, q, MOE_TILE, nq)
            los.append(lo.astype(_MXU))
            his.append(hi.astype(_MXU))
        x = jnp.concatenate(los + his, axis=1)
        hgu = jnp.dot(x, wgu_s[...], preferred_element_type=F32)
        hg = hgu[:, :EXPERT_HIDDEN]
        hid = hg * jax.nn.sigmoid(hg) * hgu[:, EXPERT_HIDDEN:]
        _pack_rows(o_ref, _dot(hid, wd_s[...]))

    @pl.when(jnp.logical_not(used))
    def _():
        o_ref[...] = jnp.zeros_like(o_ref)


def _experts(tile_e, nvalid, xsorted, wg, wu, wd, layer):
    _, E, D, Hd = wg.shape
    nq = D // (2 * LANES)
    NT = xsorted.shape[0] // (MOE_TILE * nq)
    return pl.pallas_call(
        functools.partial(_expert_kernel, nq=nq),
        out_shape=jax.ShapeDtypeStruct(xsorted.shape, U32),
        grid_spec=pltpu.PrefetchScalarGridSpec(
            num_scalar_prefetch=2,
            grid=(NT,),
            in_specs=[pl.BlockSpec((MOE_TILE * nq, LANES), lambda j, te, nv: (j, 0)),
                      pl.BlockSpec((1, 1, D, Hd), lambda j, te, nv: (layer, te[j], 0, 0)),
                      pl.BlockSpec((1, 1, D, Hd), lambda j, te, nv: (layer, te[j], 0, 0)),
                      pl.BlockSpec((1, 1, Hd, D), lambda j, te, nv: (layer, te[j], 0, 0))],
            out_specs=pl.BlockSpec((MOE_TILE * nq, LANES), lambda j, te, nv: (j, 0)),
            scratch_shapes=[pltpu.VMEM((D, 2 * Hd), _MXU), pltpu.VMEM((Hd, D), _MXU)],
        ),
        compiler_params=_params("arbitrary"),
        name="moe_experts",
    )(tile_e, nvalid, xsorted, wg, wu, wd)


def _combine_kernel(pos_ref, y_ref, x_ref, r_ref, g2b, g2c, o_ref, a0, a1, b0, b1, sem_a, sem_b,
                    *, n_lat, tc, S, nq):
    b = pl.program_id(0)
    i = pl.program_id(1)
    nt = pl.num_programs(1)
    step = b * nt + i
    last = pl.num_programs(0) * nt - 1
    sets = ((a0, a1, sem_a), (b0, b1, sem_b))

    def start_gathers(base, bufs):
        buf0, buf1, sem = bufs

        def issue(r, carry):
            dst = pl.ds(pl.multiple_of(r * nq, nq), nq)
            for k, buf in ((0, buf0), (1, buf1)):
                src = y_ref.at[pl.ds(pl.multiple_of(pos_ref[2 * (base + r) + k], nq), nq)]
                pltpu.make_async_copy(src, buf.at[dst], sem).start()
            return carry

        lax.fori_loop(0, tc, issue, 0)

    def consume(bufs):
        buf0, buf1, sem = bufs
        for buf in (buf0, buf1):
            pltpu.make_async_copy(y_ref.at[pl.ds(0, tc * nq)], buf, sem).wait()
        r = r_ref[0]
        c1, c2 = r[:, 2:3], r[:, 3:4]
        rows = i * tc + lax.broadcasted_iota(jnp.int32, (tc, 1), 0)
        lat = rows < n_lat
        half = x_ref.shape[2] // 2
        for q in range(nq):
            lo0, hi0 = _unpack_rows(buf0, q, tc, nq)
            lo1, hi1 = _unpack_rows(buf1, q, tc, nq)
            for col, y0, y1 in ((q * LANES, lo0, lo1), (half + q * LANES, hi0, hi1)):
                cs = slice(col, col + LANES)
                g2 = jnp.where(lat, g2b[0, :, cs], g2c[0, :, cs])
                o_ref[0, :, cs] = x_ref[0, :, cs] + g2 * (c1 * y0 + c2 * y1)

    @pl.when(step == 0)
    def _():
        start_gathers(0, sets[0])

    next_base = jnp.where(i + 1 < nt, b * S + (i + 1) * tc, (b + 1) * S)
    for parity in range(2):
        @pl.when((step < last) & (step % 2 == parity))
        def _():
            start_gathers(next_base, sets[1 - parity])

    for parity in range(2):
        @pl.when(step % 2 == parity)
        def _():
            consume(sets[parity])


def _combine(pos, ysorted, xs, route, mod, base, n_lat, rows):
    B, S, D = xs.shape
    nq = D // (2 * LANES)
    tc = _tile(math.gcd(rows, S), 256, SUBLANES)
    return pl.pallas_call(
        functools.partial(_combine_kernel, n_lat=n_lat, tc=tc, S=S, nq=nq),
        out_shape=jax.ShapeDtypeStruct((B, rows, D), F32),
        grid_spec=pltpu.PrefetchScalarGridSpec(
            num_scalar_prefetch=1,
            grid=(B, rows // tc),
            in_specs=[pl.BlockSpec(memory_space=pl.ANY),
                      pl.BlockSpec((1, tc, D), lambda b, i, pos: (b, i, 0)),
                      pl.BlockSpec((1, tc, ROUTE_COLS), lambda b, i, pos: (b, i, 0))]
            + _mod_specs(D, base, B, 5),
            out_specs=pl.BlockSpec((1, tc, D), lambda b, i, pos: (b, i, 0)),
            scratch_shapes=[pltpu.VMEM((tc * nq, LANES), U32) for _ in range(4)]
            + [pltpu.SemaphoreType.DMA(()), pltpu.SemaphoreType.DMA(())],
        ),
        compiler_params=_params("arbitrary", "arbitrary"),
        name="moe_combine",
    )(pos, ysorted, xs, route, mod, mod)


def _swap_perm():
    j = np.arange(QK_ROPE)
    a, h, f = j // ROPE_AXIS_DIM, (j // ROPE_FREQS) % 2, j % ROPE_FREQS
    return a * ROPE_AXIS_DIM + (1 - h) * ROPE_FREQS + f


def _rope_tables(n_lat, S):
    t = jnp.arange(n_lat)
    row = (t // GRID_W).astype(F32)
    col = (t % GRID_W).astype(F32)
    inv = ROPE_THETA ** (-jnp.arange(0, ROPE_AXIS_DIM, 2, dtype=F32) / ROPE_AXIS_DIM)
    a0, a1 = row[:, None] * inv, col[:, None] * inv
    c64 = jnp.concatenate([jnp.cos(a0), jnp.cos(a0), jnp.cos(a1), jnp.cos(a1)], axis=1)
    s64 = jnp.concatenate([-jnp.sin(a0), jnp.sin(a0), -jnp.sin(a1), jnp.sin(a1)], axis=1)
    nc = S - n_lat
    c64 = jnp.concatenate([c64, jnp.ones((nc, QK_ROPE), F32)], axis=0)
    s64 = jnp.concatenate([s64, jnp.zeros((nc, QK_ROPE), F32)], axis=0)
    ct = jnp.concatenate([jnp.ones((S, QK_NOPE), F32), c64, jnp.zeros((S, QK_ROPE), F32)], axis=1)
    st = jnp.concatenate([jnp.zeros((S, QK_NOPE), F32), s64, jnp.zeros((S, QK_ROPE), F32)], axis=1)
    return ct, st


def _head_gains(g, perm):
    z = jnp.zeros((QK_ROPE,), F32)
    ga = jnp.concatenate([g[:QK_NOPE], g[QK_NOPE:], z])
    gb = jnp.concatenate([jnp.zeros((QK_NOPE,), F32), g[QK_NOPE:][perm], z])
    return jnp.stack([ga, gb], axis=0)


def _prep_w_in(w, D, perm):
    o = np.cumsum((0, Q_LORA, KV_LORA, QK_ROPE, 2 * SGU_WIDTH, POOL_WIDTH, 3 * D))
    wq, wkv, wkr, wsgu, wpool, wgate = (w[:, o[k]:o[k + 1]] for k in range(6))
    pad = jnp.zeros((D, MLA_BLOCK - Q_LORA - KV_LORA - 2 * QK_ROPE), w.dtype)
    return jnp.concatenate([wsgu, wpool, wq, wkv, pad, wkr, wkr[:, perm], wgate], axis=1).astype(_MXU)


def _slots(route, counts, T, nq):
    e = route[:, 0:2].astype(jnp.int32)
    rank = route[:, 4:6].astype(jnp.int32)
    cnt = counts[0, N_GROUPS:N_GROUPS + N_EXPERTS].astype(jnp.int32)
    padded = (cnt + MOE_TILE - 1) // MOE_TILE * MOE_TILE
    end = jnp.cumsum(padded)
    off = end - padded
    own = e[:, :, None] == jnp.arange(N_EXPERTS, dtype=jnp.int32)
    pos = rank + jnp.sum(jnp.where(own, off, 0), axis=-1)
    n_rows = 2 * T + N_EXPERTS * MOE_TILE
    NT = n_rows // MOE_TILE
    nvalid = (end[-1] // MOE_TILE).astype(jnp.int32)
    tiles = jnp.minimum(jnp.arange(NT, dtype=jnp.int32), nvalid - 1)
    tile_e = jnp.sum((end[None, :] <= (tiles * MOE_TILE)[:, None]).astype(jnp.int32), axis=1)
    tail = nvalid + jnp.arange(N_EXPERTS, dtype=jnp.int32)
    zlo = jnp.concatenate([jnp.where(cnt > 0, (end - MOE_TILE) * nq, -1),
                           jnp.where(tail < NT, tail * (MOE_TILE * nq), -1)]).astype(jnp.int32)
    return (pos.reshape(-1) * nq).astype(jnp.int32), zlo, tile_e, nvalid.reshape(1), n_rows


def kernel(x, c, ctx, c_ctx, w_ada, b_ada, w_in, g_cq, g_ckv, w_uq, w_ukv, g_q, g_k, w_sgu, b_sgu, g_sgu,
           w_pool, s_pool, w_ao, w_bo, w_co, w_out, w_rg, b_rg, w_re, b_re, w_e_gate, w_e_up, w_e_down):
    B, n, D = x.shape
    nc = ctx.shape[1]
    S = n + nc
    T = B * S
    L = w_ada.shape[0]
    perm = _swap_perm()

    R = -(-(B + 1) // SUBLANES) * SUBLANES
    cc = jnp.concatenate([c, c_ctx[None], jnp.zeros((R - B - 1, D), F32)], axis=0)
    mod = _ada(cc, w_ada, b_ada)
    mod2 = mod.reshape(L * R, 1, 6 * D)

    ct, st = _rope_tables(n, S)
    xs = jnp.concatenate([x, ctx], axis=1)

    for i in range(L):
        base = i * R
        w_in_r = _prep_w_in(w_in[i], D, perm)
        wq = w_uq[i].reshape(Q_LORA, MLA_HEADS, QK_HEAD)
        wq = jnp.concatenate([wq, wq[:, :, QK_NOPE:][:, :, perm]], axis=2).reshape(Q_LORA, -1).astype(_MXU)
        wkv = w_ukv[i].astype(_MXU)
        bias = jnp.repeat(b_sgu[i].T, SGU_GROUP_CH, axis=1)
        w_r = jnp.concatenate([w_rg[i], jnp.moveaxis(w_re[i], 0, 1).reshape(D, N_EXPERTS),
                               jnp.zeros((D, ROUTE_COLS - N_GROUPS - N_EXPERTS), F32)], axis=1)
        b_r = jnp.concatenate([b_rg[i], b_re[i].reshape(-1),
                               jnp.zeros((ROUTE_COLS - N_GROUPS - N_EXPERTS,), F32)])[None]
        w_r_hi = w_r.astype(jnp.bfloat16)
        w_r = jnp.concatenate([w_r_hi, (w_r - w_r_hi.astype(F32)).astype(jnp.bfloat16)], axis=1)

        P = _inproj(xs, mod2, w_in_r, base, n)
        q, k, v = _mla_front(P, wq, wkv, g_cq[i][None], g_ckv[i][None],
                             _head_gains(g_q[i], perm), _head_gains(g_k[i], perm), ct, st)
        a_lat, a_ctx = _attention(q, k, v, n)
        bb = _sgu(P, g_sgu[i][None], w_sgu[i].astype(_MXU), bias)
        cp = _pool(P, w_pool[i].astype(_MXU), s_pool[i][None], n)
        xs = _merge(a_lat, a_ctx, bb, cp, P, xs, mod2, w_ao[i].astype(_MXU), w_bo[i].astype(_MXU),
                    w_co[i].astype(_MXU), w_out[i].astype(_MXU), base, n)

        h2, route, counts = _router(xs, mod2, w_r, b_r, base, n)
        pos, zlo, tile_e, nvalid, n_rows = _slots(route.reshape(T, ROUTE_COLS), counts, T, D // (2 * LANES))
        xsorted = _dispatch(pos, zlo, h2, T, n_rows)
        ysorted = _experts(tile_e, nvalid, xsorted, w_e_gate, w_e_up, w_e_down, i)
        xs = _combine(pos, ysorted, xs, route, mod2, base, n, S if i < L - 1 else n)

    return xs
```

```python
import functools
import math

import jax
import jax.numpy as jnp
import numpy as np
from jax import lax
from jax.experimental import pallas as pl
from jax.experimental.pallas import tpu as pltpu

GRID_W = 64
EPS = 1e-6
MLA_HEADS = 8
QK_NOPE = 128
QK_ROPE = 64
V_HEAD = 128
QK_HEAD = QK_NOPE + QK_ROPE
Q_LORA = 512
KV_LORA = 256
MLA_WIDTH = MLA_HEADS * V_HEAD
ROPE_THETA = 10000.0
ROPE_AXIS_DIM = QK_ROPE // 2
ROPE_FREQS = ROPE_AXIS_DIM // 2
ATTN_SCALE = 1.0 / math.sqrt(QK_HEAD)
LOG2E = math.log2(math.e)
SGU_GROUPS = 8
SGU_CHUNK = 128
SGU_GROUP_CH = 128
SGU_WIDTH = SGU_GROUPS * SGU_GROUP_CH
POOL_WINDOWS = (2, 4, 8, 16)
POOL_GROUP_CH = 256
POOL_WIDTH = len(POOL_WINDOWS) * POOL_GROUP_CH
N_GROUPS = 4
EXPERTS_PER_GROUP = 8
N_EXPERTS = N_GROUPS * EXPERTS_PER_GROUP
EXPERT_HIDDEN = 512

LANES = 128
SUBLANES = 8
HEAD_PAD = 256
VMEM_LIMIT = 56 * 1024 * 1024

MLA_BLOCK = 1024
COL_SGU = 0
COL_POOL = 2 * SGU_WIDTH
COL_MLA = COL_POOL + POOL_WIDTH
COL_GATE = COL_MLA + MLA_BLOCK

POOL_HALO = 16
ROUTE_COLS = LANES
MOE_TILE = 512
NORM_CHUNK = 32

_MXU = jnp.bfloat16
_ACT = jnp.bfloat16
F32 = jnp.float32
U32 = jnp.uint32


def _params(*sem):
    return pltpu.CompilerParams(dimension_semantics=sem, vmem_limit_bytes=VMEM_LIMIT)


def _tile(total, cap, mult=LANES):
    best = None
    for t in range(mult, min(total, cap) + 1, mult):
        if total % t == 0:
            best = t
    assert best is not None, (total, cap, mult)
    return best


def _dot(a, b):
    return jnp.dot(a.astype(_MXU), b.astype(_MXU), preferred_element_type=F32)


def _ada_kernel(c_ref, w_ref, b_ref, o_ref):
    c = c_ref[...]
    s = c * jax.nn.sigmoid(c)
    o_ref[0] = _dot(s, w_ref[0]) + b_ref[0]


def _ada(cc, w_ada, b_ada):
    L, D, N = w_ada.shape
    R = cc.shape[0]
    tn = _tile(N, 2048)
    return pl.pallas_call(
        _ada_kernel,
        out_shape=jax.ShapeDtypeStruct((L, R, N), F32),
        grid=(L, N // tn),
        in_specs=[
            pl.BlockSpec((R, D), lambda l, j: (0, 0)),
            pl.BlockSpec((1, D, tn), lambda l, j: (l, 0, j)),
            pl.BlockSpec((1, 1, tn), lambda l, j: (l, 0, j)),
        ],
        out_specs=pl.BlockSpec((1, R, tn), lambda l, j: (l, 0, j)),
        compiler_params=_params("parallel", "parallel"),
        name="ada_mod",
    )(cc, w_ada, b_ada.reshape(L, 1, N))


def _norm_mod(x, row0, n_lat, sh_b, sc_b, sh_c, sc_c):
    ms = jnp.mean(x * x, axis=-1, keepdims=True)
    xn = x * lax.rsqrt(ms + EPS)
    rows = row0 + lax.broadcasted_iota(jnp.int32, (x.shape[0], 1), 0)
    lat = rows < n_lat
    sc = jnp.where(lat, sc_b, sc_c)
    sh = jnp.where(lat, sh_b, sh_c)
    return xn * (1.0 + sc) + sh


def _mod_specs(D, base, B, col):
    return [pl.BlockSpec((1, 1, D), lambda *g: (base + g[0], 0, col)),
            pl.BlockSpec((1, 1, D), lambda *g: (base + B, 0, col))]


def _inproj_kernel(x_ref, shb, shc, scb, scc, w_ref, o_ref, h_ref, *, n_lat, tm):
    @pl.when(pl.program_id(2) == 0)
    def _():
        def chunk(c, carry):
            r0 = pl.multiple_of(c * NORM_CHUNK, NORM_CHUNK)
            h = _norm_mod(x_ref[0, pl.ds(r0, NORM_CHUNK), :], pl.program_id(1) * tm + r0, n_lat,
                          shb[0], scb[0], shc[0], scc[0])
            h_ref[pl.ds(r0, NORM_CHUNK), :] = h.astype(h_ref.dtype)
            return carry
        lax.fori_loop(0, tm // NORM_CHUNK, chunk, 0)
    o_ref[0] = jnp.dot(h_ref[...], w_ref[0], preferred_element_type=F32).astype(o_ref.dtype)


def _inproj(xs, mod, w, layer, base, n_lat):
    B, S, D = xs.shape
    NP = w.shape[2]
    tm = _tile(S, 1024)
    tn = _tile(NP, 2048)
    return pl.pallas_call(
        functools.partial(_inproj_kernel, n_lat=n_lat, tm=tm),
        out_shape=jax.ShapeDtypeStruct((B, S, NP), _ACT),
        grid=(B, S // tm, NP // tn),
        in_specs=[pl.BlockSpec((1, tm, D), lambda b, i, j: (b, i, 0))]
        + _mod_specs(D, base, B, 0) + _mod_specs(D, base, B, 1)
        + [pl.BlockSpec((1, D, tn), lambda b, i, j: (layer, 0, j))],
        out_specs=pl.BlockSpec((1, tm, tn), lambda b, i, j: (b, i, j)),
        scratch_shapes=[pltpu.VMEM((tm, D), _MXU)],
        compiler_params=_params("parallel", "parallel", "arbitrary"),
        name="in_proj",
    )(xs, mod, mod, mod, mod, w)


def _rope_half(b, gct, gst):
    return b * gct + pltpu.roll(b, QK_ROPE, axis=1) * gst


def _sumsq_rope(b):
    lane = lax.broadcasted_iota(jnp.int32, (1, LANES), 1)
    return jnp.sum(jnp.where(lane < QK_ROPE, b * b, 0.0), axis=-1, keepdims=True)


def _mla_kernel(p_ref, wq_ref, wkv_ref, gcq_ref, gckv_ref, gq_ref, gk_ref, ct_ref, st_ref,
                q_ref, k_ref, v_ref):
    p = p_ref[0].astype(F32)
    pq = p[:, :Q_LORA]
    pkv = p[:, Q_LORA:Q_LORA + KV_LORA]
    kr = p[:, MLA_BLOCK - LANES:]
    cq = pq * lax.rsqrt(jnp.mean(pq * pq, axis=-1, keepdims=True) + EPS) * gcq_ref[...]
    ckv = pkv * lax.rsqrt(jnp.mean(pkv * pkv, axis=-1, keepdims=True) + EPS) * gckv_ref[...]
    qf = _dot(cq, wq_ref[...])
    kvf = _dot(ckv, wkv_ref[...])
    ct = ct_ref[...]
    st = st_ref[...]
    gq = gq_ref[...]
    gk = gk_ref[...]
    gq_n, q_gct, q_gst = gq[0:1, :QK_NOPE], gq[0:1, QK_NOPE:] * ct, gq[1:2, QK_NOPE:] * st
    gk_n = gk[0:1, :QK_NOPE]
    k_rot = _rope_half(kr, gk[0:1, QK_NOPE:] * ct, gk[1:2, QK_NOPE:] * st)
    kr_ss = _sumsq_rope(kr)
    for h in range(MLA_HEADS):
        lo = h * HEAD_PAD
        qa, qb = qf[:, lo:lo + QK_NOPE], qf[:, lo + QK_NOPE:lo + HEAD_PAD]
        rq = lax.rsqrt((jnp.sum(qa * qa, axis=-1, keepdims=True) + _sumsq_rope(qb)) * (1.0 / QK_HEAD) + EPS)
        rq = rq * (ATTN_SCALE * LOG2E)
        q_ref[0, :, lo:lo + QK_NOPE] = (qa * gq_n * rq).astype(q_ref.dtype)
        q_ref[0, :, lo + QK_NOPE:lo + HEAD_PAD] = (_rope_half(qb, q_gct, q_gst) * rq).astype(q_ref.dtype)
        ka = kvf[:, lo:lo + QK_NOPE]
        rk = lax.rsqrt((jnp.sum(ka * ka, axis=-1, keepdims=True) + kr_ss) * (1.0 / QK_HEAD) + EPS)
        k_ref[0, :, lo:lo + QK_NOPE] = (ka * gk_n * rk).astype(k_ref.dtype)
        k_ref[0, :, lo + QK_NOPE:lo + HEAD_PAD] = (k_rot * rk).astype(k_ref.dtype)
        v_ref[0, :, h * V_HEAD:(h + 1) * V_HEAD] = kvf[:, lo + QK_NOPE:lo + HEAD_PAD].astype(v_ref.dtype)


def _mla_front(P, wq, wkv, gcq, gckv, gq2, gk2, ct, st):
    B, S, _ = P.shape
    tm = _tile(S, 512)
    HW = MLA_HEADS * HEAD_PAD
    cblk = COL_MLA // MLA_BLOCK
    full = lambda shape: pl.BlockSpec(shape, lambda b, i: (0,) * len(shape))
    return pl.pallas_call(
        _mla_kernel,
        out_shape=(jax.ShapeDtypeStruct((B, S, HW), _ACT),
                   jax.ShapeDtypeStruct((B, S, HW), _ACT),
                   jax.ShapeDtypeStruct((B, S, MLA_WIDTH), _ACT)),
        grid=(B, S // tm),
        in_specs=[pl.BlockSpec((1, tm, MLA_BLOCK), lambda b, i: (b, i, cblk)),
                  full(wq.shape), full(wkv.shape), full(gcq.shape), full(gckv.shape),
                  full(gq2.shape), full(gk2.shape),
                  pl.BlockSpec((tm, LANES), lambda b, i: (i, QK_NOPE // LANES)),
                  pl.BlockSpec((tm, LANES), lambda b, i: (i, QK_NOPE // LANES))],
        out_specs=(pl.BlockSpec((1, tm, HW), lambda b, i: (b, i, 0)),
                   pl.BlockSpec((1, tm, HW), lambda b, i: (b, i, 0)),
                   pl.BlockSpec((1, tm, MLA_WIDTH), lambda b, i: (b, i, 0))),
        compiler_params=_params("parallel", "parallel"),
        name="mla_front",
    )(P, wq, wkv, gcq, gckv, gq2, gk2, ct, st)


def _attn_kernel(q_ref, k_ref, v_ref, o_ref):
    nk = k_ref.shape[1]
    ones = (lax.broadcasted_iota(jnp.int32, (nk, V_HEAD), 1) == 0).astype(v_ref.dtype)
    for h in range(MLA_HEADS):
        q = q_ref[0, :, h * HEAD_PAD:(h + 1) * HEAD_PAD]
        k = k_ref[0, :, h * HEAD_PAD:(h + 1) * HEAD_PAD]
        v = jnp.concatenate([v_ref[0, :, h * V_HEAD:(h + 1) * V_HEAD], ones], axis=1)
        s = lax.dot_general(q, k, (((1,), (1,)), ((), ())), preferred_element_type=F32)
        p = jnp.exp2(s - jnp.max(s, axis=-1, keepdims=True))
        o = jnp.dot(p.astype(v.dtype), v, preferred_element_type=F32)
        o_ref[0, :, h * V_HEAD:(h + 1) * V_HEAD] = (
            o[:, :V_HEAD] / o[:, V_HEAD:V_HEAD + 1]).astype(o_ref.dtype)


def _attention(q, k, v, n_lat):
    B, S, HW = q.shape
    nc = S - n_lat
    assert n_lat % nc == 0, (n_lat, nc)
    tq = _tile(n_lat, 512)
    a_lat = pl.pallas_call(
        _attn_kernel,
        out_shape=jax.ShapeDtypeStruct((B, n_lat, MLA_WIDTH), _ACT),
        grid=(B, n_lat // tq),
        in_specs=[pl.BlockSpec((1, tq, HW), lambda b, i: (b, i, 0)),
                  pl.BlockSpec((1, S, HW), lambda b, i: (b, 0, 0)),
                  pl.BlockSpec((1, S, MLA_WIDTH), lambda b, i: (b, 0, 0))],
        out_specs=pl.BlockSpec((1, tq, MLA_WIDTH), lambda b, i: (b, i, 0)),
        compiler_params=_params("parallel", "arbitrary"),
        name="attention",
    )(q, k, v)
    cblk = n_lat // nc
    ctx_rows = lambda w: pl.BlockSpec((1, nc, w), lambda b: (b, cblk, 0))
    a_ctx = pl.pallas_call(
        _attn_kernel,
        out_shape=jax.ShapeDtypeStruct((B, nc, MLA_WIDTH), _ACT),
        grid=(B,),
        in_specs=[ctx_rows(HW), ctx_rows(HW), ctx_rows(MLA_WIDTH)],
        out_specs=pl.BlockSpec((1, nc, MLA_WIDTH), lambda b: (b, 0, 0)),
        compiler_params=_params("parallel"),
        name="attention_ctx",
    )(q, k, v)
    return a_lat, a_ctx


def _gelu_tanh(x):
    return 0.5 * x * (1.0 + jnp.tanh(math.sqrt(2.0 / math.pi) * (x + 0.044715 * (x * x * x))))


def _sgu_kernel(p_ref, g_ref, w_ref, b_ref, o_ref, *, tm):
    gp = _gelu_tanh(p_ref[0].astype(F32))
    u = gp[:, :SGU_WIDTH]
    v = gp[:, SGU_WIDTH:]
    v = v * lax.rsqrt(jnp.mean(v * v, axis=-1, keepdims=True) + EPS) * g_ref[...]
    vb = v.astype(_MXU)
    bias = b_ref[...]
    for c in range(tm // SGU_CHUNK):
        rs = slice(c * SGU_CHUNK, (c + 1) * SGU_CHUNK)
        for g in range(SGU_GROUPS):
            cs = slice(g * SGU_GROUP_CH, (g + 1) * SGU_GROUP_CH)
            mix = jnp.dot(w_ref[g], vb[rs, cs], preferred_element_type=F32) + bias[:, cs]
            o_ref[0, rs, cs] = (u[rs, cs] * mix).astype(o_ref.dtype)


def _sgu(P, g_sgu, w_sgu, bias):
    B, S, _ = P.shape
    tm = _tile(S, 512)
    full = lambda shape: pl.BlockSpec(shape, lambda b, i: (0,) * len(shape))
    return pl.pallas_call(
        functools.partial(_sgu_kernel, tm=tm),
        out_shape=jax.ShapeDtypeStruct((B, S, SGU_WIDTH), _ACT),
        grid=(B, S // tm),
        in_specs=[pl.BlockSpec((1, tm, 2 * SGU_WIDTH), lambda b, i: (b, i, COL_SGU // (2 * SGU_WIDTH))),
                  full(g_sgu.shape), full(w_sgu.shape), full(bias.shape)],
        out_specs=pl.BlockSpec((1, tm, SGU_WIDTH), lambda b, i: (b, i, 0)),
        compiler_params=_params("parallel", "parallel"),
        name="spatial_gating",
    )(P, g_sgu, w_sgu, bias)


def _window_sum(x, g):
    n, C = x.shape
    z = jnp.zeros((POOL_HALO, C), F32)
    xe = jnp.concatenate([z, x, z], axis=0)
    ne = n + 2 * POOL_HALO
    w = xe + pltpu.roll(xe, 1, axis=0)
    for s in range(g):
        k = 1 << s
        w = pltpu.roll(w, k, axis=0) + pltpu.roll(w, ne - k, axis=0)
    return w[POOL_HALO:POOL_HALO + n]


def _pool_kernel(p_ref, w_ref, s_ref, o_ref, *, n_lat):
    S = p_ref.shape[1]
    for lo, n in ((0, n_lat), (n_lat, S - n_lat)):
        t = lax.broadcasted_iota(jnp.int32, (n, 1), 0)
        for g, win in enumerate(POOL_WINDOWS):
            cs = slice(g * POOL_GROUP_CH, (g + 1) * POOL_GROUP_CH)
            x = p_ref[0, lo:lo + n, cs].astype(F32)
            half = win // 2
            cnt = (jnp.minimum(t + half, n) - jnp.maximum(t - half, 0)).astype(F32)
            m = _window_sum(x, g) / cnt - x
            y = _dot(m, w_ref[g]) * s_ref[:, cs]
            o_ref[0, lo:lo + n, cs] = y.astype(o_ref.dtype)


def _pool(P, w_pool, s_pool, n_lat):
    B, S, _ = P.shape
    full = lambda shape: pl.BlockSpec(shape, lambda b: (0,) * len(shape))
    return pl.pallas_call(
        functools.partial(_pool_kernel, n_lat=n_lat),
        out_shape=jax.ShapeDtypeStruct((B, S, POOL_WIDTH), _ACT),
        grid=(B,),
        in_specs=[pl.BlockSpec((1, S, POOL_WIDTH), lambda b: (b, 0, COL_POOL // POOL_WIDTH)),
                  full(w_pool.shape), full(s_pool.shape)],
        out_specs=pl.BlockSpec((1, S, POOL_WIDTH), lambda b: (b, 0, 0)),
        compiler_params=_params("parallel"),
        name="multiscale_pool",
    )(P, w_pool, s_pool)


def _merge_kernel(al_ref, ac_ref, b_ref, c_ref, ga_ref, gb_ref, gc_ref, x_ref, g1b, g1c,
                  wa_ref, wb_ref, wc_ref, wo_ref, o_ref, *, n_lat, tm):
    def gate(ref):
        return jax.nn.sigmoid(ref[0].astype(F32))
    a = jnp.where(pl.program_id(1) * tm < n_lat, al_ref[0], ac_ref[0])
    y = gate(ga_ref) * jnp.dot(a, wa_ref[...], preferred_element_type=F32)
    y = y + gate(gb_ref) * jnp.dot(b_ref[0], wb_ref[...], preferred_element_type=F32)
    y = y + gate(gc_ref) * jnp.dot(c_ref[0], wc_ref[...], preferred_element_type=F32)
    out = _dot(y, wo_ref[...])
    rows = pl.program_id(1) * tm + lax.broadcasted_iota(jnp.int32, (tm, 1), 0)
    g1 = jnp.where(rows < n_lat, g1b[0], g1c[0])
    o_ref[0] = x_ref[0] + g1 * out


def _merge(a_lat, a_ctx, b, cp, P, xs, mod, wa, wb, wc, wo, base, n_lat):
    B, S, D = xs.shape
    tm = _tile(math.gcd(n_lat, S - n_lat), 256)
    gblk = COL_GATE // D
    nl = n_lat // tm
    tok = lambda w: pl.BlockSpec((1, tm, w), lambda bb, i: (bb, i, 0))
    att_lat = pl.BlockSpec((1, tm, MLA_WIDTH), lambda bb, i: (bb, jnp.minimum(i, nl - 1), 0))
    att_ctx = pl.BlockSpec((1, tm, MLA_WIDTH), lambda bb, i: (bb, jnp.maximum(i - nl, 0), 0))
    gate = lambda k: pl.BlockSpec((1, tm, D), lambda bb, i: (bb, i, gblk + k))
    wres = lambda shape: pl.BlockSpec(shape, lambda bb, i: (0, 0), pipeline_mode=pl.Buffered(1))
    return pl.pallas_call(
        functools.partial(_merge_kernel, n_lat=n_lat, tm=tm),
        out_shape=jax.ShapeDtypeStruct((B, S, D), F32),
        grid=(B, S // tm),
        in_specs=[att_lat, att_ctx, tok(SGU_WIDTH), tok(POOL_WIDTH), gate(0), gate(1), gate(2), tok(D)]
        + _mod_specs(D, base, B, 2)
        + [wres(wa.shape), wres(wb.shape), wres(wc.shape), wres(wo.shape)],
        out_specs=tok(D),
        compiler_params=_params("parallel", "parallel"),
        name="merge_out",
    )(a_lat, a_ctx, b, cp, P, P, P, xs, mod, mod, wa, wb, wc, wo)


def _pack_rows(o_ref, y):
    rows, half = y.shape[0], y.shape[1] // 2
    nq = half // LANES
    lo = lax.bitcast_convert_type(y[:, :half].astype(jnp.bfloat16).astype(F32), U32)
    hi = lax.bitcast_convert_type(y[:, half:].astype(jnp.bfloat16).astype(F32), U32)
    w = hi | (lo >> 16)
    for q in range(nq):
        o_ref[pl.ds(q, rows, stride=nq), :] = w[:, q * LANES:(q + 1) * LANES]


def _unpack_rows(ref, q, rows, nq):
    w = ref[pl.ds(q, rows, stride=nq), :]
    lo = lax.bitcast_convert_type(w << 16, F32)
    hi = lax.bitcast_convert_type(w & jnp.uint32(0xFFFF0000), F32)
    return lo, hi


def _router_kernel(x_ref, shb, shc, scb, scc, w_ref, b_ref, h_ref, r_ref, cnt_ref, *, n_lat, tm):
    first = (pl.program_id(0) == 0) & (pl.program_id(1) == 0)

    @pl.when(first)
    def _():
        cnt_ref[...] = jnp.zeros_like(cnt_ref)

    h = _norm_mod(x_ref[0], pl.program_id(1) * tm, n_lat, shb[0], scb[0], shc[0], scc[0])
    _pack_rows(h_ref, h)
    h_hi = h.astype(jnp.bfloat16)
    h_lo = (h - h_hi.astype(F32)).astype(jnp.bfloat16)
    t = jnp.dot(h_hi, w_ref[...], preferred_element_type=F32)
    logits = (t[:, :ROUTE_COLS] + t[:, ROUTE_COLS:]
              + jnp.dot(h_lo, w_ref[:, :ROUTE_COLS], preferred_element_type=F32) + b_ref[...])
    lane = lax.broadcasted_iota(jnp.int32, (tm, ROUTE_COLS), 1)
    ninf = -jnp.inf
    lg = jnp.where(lane < N_GROUPS, logits, ninf)
    mg = jnp.max(lg, axis=-1, keepdims=True)
    g_w = 1.0 / jnp.sum(jnp.exp(lg - mg), axis=-1, keepdims=True)
    g_i = jnp.min(jnp.where(lg == mg, lane, ROUTE_COLS), axis=-1, keepdims=True)
    lo = N_GROUPS + g_i * EXPERTS_PER_GROUP
    le = jnp.where((lane >= lo) & (lane < lo + EXPERTS_PER_GROUP), logits, ninf)
    m1 = jnp.max(le, axis=-1, keepdims=True)
    i1 = jnp.min(jnp.where(le == m1, lane, ROUTE_COLS), axis=-1, keepdims=True)
    le2 = jnp.where(lane == i1, ninf, le)
    m2 = jnp.max(le2, axis=-1, keepdims=True)
    i2 = jnp.min(jnp.where(le2 == m2, lane, ROUTE_COLS), axis=-1, keepdims=True)
    e21 = jnp.exp(m2 - m1)
    w1 = 1.0 / (1.0 + e21)
    w2 = e21 * w1
    onehot = jnp.where((lane == i1) | (lane == i2), 1.0, 0.0)
    ti = lax.broadcasted_iota(jnp.int32, (tm, tm), 0)
    tj = lax.broadcasted_iota(jnp.int32, (tm, tm), 1)
    tri = jnp.where(ti > tj, 1.0, 0.0).astype(jnp.bfloat16)
    before = jnp.dot(tri, onehot.astype(jnp.bfloat16), preferred_element_type=F32) + cnt_ref[...]
    r1 = jnp.sum(jnp.where(lane == i1, before, 0.0), axis=-1, keepdims=True)
    r2 = jnp.sum(jnp.where(lane == i2, before, 0.0), axis=-1, keepdims=True)
    cnt_ref[...] += jnp.sum(onehot, axis=0, keepdims=True)
    out = jnp.where(lane == 0, (i1 - N_GROUPS).astype(F32), 0.0)
    out = jnp.where(lane == 1, (i2 - N_GROUPS).astype(F32), out)
    out = jnp.where(lane == 2, g_w * w1, out)
    out = jnp.where(lane == 3, g_w * w2, out)
    out = jnp.where(lane == 4, r1, out)
    out = jnp.where(lane == 5, r2, out)
    r_ref[0] = out


def _router(xs, mod, w_r, b_r, base, n_lat):
    B, S, D = xs.shape
    tm = _tile(S, 512)
    Q2 = D // (2 * LANES)
    nt = S // tm
    full = lambda shape: pl.BlockSpec(shape, lambda b, i: (0,) * len(shape))
    return pl.pallas_call(
        functools.partial(_router_kernel, n_lat=n_lat, tm=tm),
        out_shape=(jax.ShapeDtypeStruct((B * S * Q2, LANES), U32),
                   jax.ShapeDtypeStruct((B, S, ROUTE_COLS), F32),
                   jax.ShapeDtypeStruct((1, ROUTE_COLS), F32)),
        grid=(B, S // tm),
        in_specs=[pl.BlockSpec((1, tm, D), lambda b, i: (b, i, 0))]
        + _mod_specs(D, base, B, 3) + _mod_specs(D, base, B, 4)
        + [full(w_r.shape), full(b_r.shape)],
        out_specs=(pl.BlockSpec((tm * Q2, LANES), lambda b, i: (b * nt + i, 0)),
                   pl.BlockSpec((1, tm, ROUTE_COLS), lambda b, i: (b, i, 0)),
                   pl.BlockSpec((1, ROUTE_COLS), lambda b, i: (0, 0))),
        compiler_params=_params("arbitrary", "arbitrary"),
        name="moe_router",
    )(xs, mod, mod, mod, mod, w_r, b_r)


def _dispatch_kernel(pos_ref, zlo_ref, h_ref, o_ref, zero_ref, stage, ld_sem, sc_sem, zsem, *, td, nq):
    c = pl.program_id(0)
    n = pl.num_programs(0)
    rows = td * nq
    slot = c % DISPATCH_SLOTS

    def load(chunk, s):
        src = h_ref.at[pl.ds(pl.multiple_of(chunk * rows, rows), rows)]
        return pltpu.make_async_copy(src, stage.at[s], ld_sem.at[s])

    def row_copies_of(s):
        return pltpu.make_async_copy(h_ref.at[pl.ds(0, 2 * rows)], o_ref.at[pl.ds(0, 2 * rows)], sc_sem.at[s])

    def zero_fill(e):
        start = pl.multiple_of(zlo_ref[e], nq)
        return pltpu.make_async_copy(zero_ref, o_ref.at[pl.ds(start, MOE_TILE * nq)], zsem)

    @pl.when(c == 0)
    def _():
        zero_ref[...] = jnp.zeros_like(zero_ref)
        for e in range(2 * N_EXPERTS):
            @pl.when(zlo_ref[e] >= 0)
            def _():
                zero_fill(e).start()
        for e in range(2 * N_EXPERTS):
            @pl.when(zlo_ref[e] >= 0)
            def _():
                zero_fill(e).wait()

        load(0, 0).start()

    @pl.when(c >= 2)
    def _():
        row_copies_of((c + 1) % DISPATCH_SLOTS).wait()

    @pl.when(c + 1 < n)
    def _():
        load(c + 1, (c + 1) % DISPATCH_SLOTS).start()

    load(c, slot).wait()

    def issue(r, carry):
        src = stage.at[slot, pl.ds(pl.multiple_of(r * nq, nq), nq)]
        for k in range(2):
            dst = o_ref.at[pl.ds(pl.multiple_of(pos_ref[2 * (c * td + r) + k], nq), nq)]
            pltpu.make_async_copy(src, dst, sc_sem.at[slot]).start(priority=k)
        return carry

    lax.fori_loop(0, td, issue, 0)

    @pl.when(c == n - 1)
    def _():
        @pl.when(c >= 1)
        def _():
            row_copies_of((c + DISPATCH_SLOTS - 1) % DISPATCH_SLOTS).wait()
        row_copies_of(slot).wait()


DISPATCH_SLOTS = 3


def _dispatch(pos, zlo, h2p, T, n_rows):
    nq = h2p.shape[0] // T
    td = _tile(T // 2, 512, SUBLANES)
    return pl.pallas_call(
        functools.partial(_dispatch_kernel, td=td, nq=nq),
        out_shape=jax.ShapeDtypeStruct((n_rows * nq, LANES), h2p.dtype),
        grid_spec=pltpu.PrefetchScalarGridSpec(
            num_scalar_prefetch=2,
            grid=(T // td,),
            in_specs=[pl.BlockSpec(memory_space=pl.ANY)],
            out_specs=pl.BlockSpec(memory_space=pl.ANY),
            scratch_shapes=[pltpu.VMEM((MOE_TILE * nq, LANES), h2p.dtype),
                            pltpu.VMEM((DISPATCH_SLOTS, td * nq, LANES), h2p.dtype),
                            pltpu.SemaphoreType.DMA((DISPATCH_SLOTS,)),
                            pltpu.SemaphoreType.DMA((DISPATCH_SLOTS,)),
                            pltpu.SemaphoreType.DMA(())],
        ),
        compiler_params=pltpu.CompilerParams(dimension_semantics=("arbitrary",),
                                             vmem_limit_bytes=VMEM_LIMIT, has_side_effects=True),
        name="moe_dispatch",
    )(pos, zlo, h2p)


def _expert_kernel(te_ref, nv_ref, x_ref, wg_ref, wu_ref, wd_ref, o_ref, wgu_s, wd_s, *, nq):
    j = pl.program_id(0)
    used = j < nv_ref[0]
    new_expert = (j == 0) | (te_ref[j] != te_ref[jnp.maximum(j - 1, 0)])

    @pl.when(used & new_expert)
    def _():
        wgu_s[:, :EXPERT_HIDDEN] = wg_ref[0, 0].astype(wgu_s.dtype)
        wgu_s[:, EXPERT_HIDDEN:] = wu_ref[0, 0].astype(wgu_s.dtype)
        wd_s[...] = wd_ref[0, 0].astype(wd_s.dtype)

    @pl.when(used)
    def _():
        los, his = [], []
        for q in range(nq):
            lo, hi = _unpack_rows(x_ref, q, MOE_TILE, nq)
            los.append(lo.astype(_MXU))
            his.append(hi.astype(_MXU))
        x = jnp.concatenate(los + his, axis=1)
        hgu = jnp.dot(x, wgu_s[...], preferred_element_type=F32)
        hg = hgu[:, :EXPERT_HIDDEN]
        hid = hg * jax.nn.sigmoid(hg) * hgu[:, EXPERT_HIDDEN:]
        _pack_rows(o_ref, _dot(hid, wd_s[...]))

    @pl.when(jnp.logical_not(used))
    def _():
        o_ref[...] = jnp.zeros_like(o_ref)


def _experts(tile_e, nvalid, xsorted, wg, wu, wd, layer):
    _, E, D, Hd = wg.shape
    nq = D // (2 * LANES)
    NT = xsorted.shape[0] // (MOE_TILE * nq)
    return pl.pallas_call(
        functools.partial(_expert_kernel, nq=nq),
        out_shape=jax.ShapeDtypeStruct(xsorted.shape, U32),
        grid_spec=pltpu.PrefetchScalarGridSpec(
            num_scalar_prefetch=2,
            grid=(NT,),
            in_specs=[pl.BlockSpec((MOE_TILE * nq, LANES), lambda j, te, nv: (j, 0)),
                      pl.BlockSpec((1, 1, D, Hd), lambda j, te, nv: (layer, te[j], 0, 0)),
                      pl.BlockSpec((1, 1, D, Hd), lambda j, te, nv: (layer, te[j], 0, 0)),
                      pl.BlockSpec((1, 1, Hd, D), lambda j, te, nv: (layer, te[j], 0, 0))],
            out_specs=pl.BlockSpec((MOE_TILE * nq, LANES), lambda j, te, nv: (j, 0)),
            scratch_shapes=[pltpu.VMEM((D, 2 * Hd), _MXU), pltpu.VMEM((Hd, D), _MXU)],
        ),
        compiler_params=_params("arbitrary"),
        name="moe_experts",
    )(tile_e, nvalid, xsorted, wg, wu, wd)


def _combine_kernel(pos_ref, y_ref, x_ref, r_ref, g2b, g2c, o_ref, a0, a1, b0, b1, sem_a, sem_b,
                    *, n_lat, tc, S, nq):
    b = pl.program_id(0)
    i = pl.program_id(1)
    nt = pl.num_programs(1)
    step = b * nt + i
    last = pl.num_programs(0) * nt - 1
    sets = ((a0, a1, sem_a), (b0, b1, sem_b))

    def start_gathers(base, bufs):
        buf0, buf1, sem = bufs

        def issue(r, carry):
            dst = pl.ds(pl.multiple_of(r * nq, nq), nq)
            for k, buf in ((0, buf0), (1, buf1)):
                src = y_ref.at[pl.ds(pl.multiple_of(pos_ref[2 * (base + r) + k], nq), nq)]
                pltpu.make_async_copy(src, buf.at[dst], sem).start(priority=k)
            return carry

        lax.fori_loop(0, tc, issue, 0)

    def consume(bufs):
        buf0, buf1, sem = bufs
        for buf in (buf0, buf1):
            pltpu.make_async_copy(y_ref.at[pl.ds(0, tc * nq)], buf, sem).wait()
        r = r_ref[0]
        c1, c2 = r[:, 2:3], r[:, 3:4]
        rows = i * tc + lax.broadcasted_iota(jnp.int32, (tc, 1), 0)
        lat = rows < n_lat
        half = x_ref.shape[2] // 2
        for q in range(nq):
            lo0, hi0 = _unpack_rows(buf0, q, tc, nq)
            lo1, hi1 = _unpack_rows(buf1, q, tc, nq)
            for col, y0, y1 in ((q * LANES, lo0, lo1), (half + q * LANES, hi0, hi1)):
                cs = slice(col, col + LANES)
                g2 = jnp.where(lat, g2b[0, :, cs], g2c[0, :, cs])
                o_ref[0, :, cs] = x_ref[0, :, cs] + g2 * (c1 * y0 + c2 * y1)

    @pl.when(step == 0)
    def _():
        start_gathers(0, sets[0])

    next_base = jnp.where(i + 1 < nt, b * S + (i + 1) * tc, (b + 1) * S)
    for parity in range(2):
        @pl.when((step < last) & (step % 2 == parity))
        def _():
            start_gathers(next_base, sets[1 - parity])

    for parity in range(2):
        @pl.when(step % 2 == parity)
        def _():
            consume(sets[parity])


def _combine(pos, ysorted, xs, route, mod, base, n_lat, rows):
    B, S, D = xs.shape
    nq = D // (2 * LANES)
    tc = _tile(math.gcd(rows, S), 256, SUBLANES)
    return pl.pallas_call(
        functools.partial(_combine_kernel, n_lat=n_lat, tc=tc, S=S, nq=nq),
        out_shape=jax.ShapeDtypeStruct((B, rows, D), F32),
        grid_spec=pltpu.PrefetchScalarGridSpec(
            num_scalar_prefetch=1,
            grid=(B, rows // tc),
            in_specs=[pl.BlockSpec(memory_space=pl.ANY),
                      pl.BlockSpec((1, tc, D), lambda b, i, pos: (b, i, 0)),
                      pl.BlockSpec((1, tc, ROUTE_COLS), lambda b, i, pos: (b, i, 0))]
            + _mod_specs(D, base, B, 5),
            out_specs=pl.BlockSpec((1, tc, D), lambda b, i, pos: (b, i, 0)),
            scratch_shapes=[pltpu.VMEM((tc * nq, LANES), U32) for _ in range(4)]
            + [pltpu.SemaphoreType.DMA(()), pltpu.SemaphoreType.DMA(())],
        ),
        compiler_params=_params("arbitrary", "arbitrary"),
        name="moe_combine",
    )(pos, ysorted, xs, route, mod, mod)


def _swap_perm():
    j = np.arange(QK_ROPE)
    a, h, f = j // ROPE_AXIS_DIM, (j // ROPE_FREQS) % 2, j % ROPE_FREQS
    return a * ROPE_AXIS_DIM + (1 - h) * ROPE_FREQS + f


def _rope_tables(n_lat, S):
    t = jnp.arange(n_lat)
    row = (t // GRID_W).astype(F32)
    col = (t % GRID_W).astype(F32)
    inv = ROPE_THETA ** (-jnp.arange(0, ROPE_AXIS_DIM, 2, dtype=F32) / ROPE_AXIS_DIM)
    a0, a1 = row[:, None] * inv, col[:, None] * inv
    c64 = jnp.concatenate([jnp.cos(a0), jnp.cos(a0), jnp.cos(a1), jnp.cos(a1)], axis=1)
    s64 = jnp.concatenate([-jnp.sin(a0), jnp.sin(a0), -jnp.sin(a1), jnp.sin(a1)], axis=1)
    nc = S - n_lat
    c64 = jnp.concatenate([c64, jnp.ones((nc, QK_ROPE), F32)], axis=0)
    s64 = jnp.concatenate([s64, jnp.zeros((nc, QK_ROPE), F32)], axis=0)
    ct = jnp.concatenate([jnp.ones((S, QK_NOPE), F32), c64, jnp.zeros((S, QK_ROPE), F32)], axis=1)
    st = jnp.concatenate([jnp.zeros((S, QK_NOPE), F32), s64, jnp.zeros((S, QK_ROPE), F32)], axis=1)
    return ct, st


def _head_gains(g, perm):
    z = jnp.zeros((QK_ROPE,), F32)
    ga = jnp.concatenate([g[:QK_NOPE], g[QK_NOPE:], z])
    gb = jnp.concatenate([jnp.zeros((QK_NOPE,), F32), g[QK_NOPE:][perm], z])
    return jnp.stack([ga, gb], axis=0)


def _prep_w_in(w, D, perm):
    o = np.cumsum((0, Q_LORA, KV_LORA, QK_ROPE, 2 * SGU_WIDTH, POOL_WIDTH, 3 * D))
    wq, wkv, wkr, wsgu, wpool, wgate = (w[:, :, o[k]:o[k + 1]].astype(_MXU) for k in range(6))
    pad = jnp.zeros(w.shape[:2] + (MLA_BLOCK - Q_LORA - KV_LORA - 2 * QK_ROPE,), _MXU)
    return jnp.concatenate([wsgu, wpool, wq, wkv, pad, wkr, wkr[:, :, perm], wgate], axis=2)


def _slots(route, counts, T, nq):
    e = route[:, 0:2].astype(jnp.int32)
    rank = route[:, 4:6].astype(jnp.int32)
    cnt = counts[0, N_GROUPS:N_GROUPS + N_EXPERTS].astype(jnp.int32)
    padded = (cnt + MOE_TILE - 1) // MOE_TILE * MOE_TILE
    end = jnp.cumsum(padded)
    off = end - padded
    own = e[:, :, None] == jnp.arange(N_EXPERTS, dtype=jnp.int32)
    pos = rank + jnp.sum(jnp.where(own, off, 0), axis=-1)
    n_rows = 2 * T + N_EXPERTS * MOE_TILE
    NT = n_rows // MOE_TILE
    nvalid = (end[-1] // MOE_TILE).astype(jnp.int32)
    tiles = jnp.minimum(jnp.arange(NT, dtype=jnp.int32), nvalid - 1)
    tile_e = jnp.sum((end[None, :] <= (tiles * MOE_TILE)[:, None]).astype(jnp.int32), axis=1)
    tail = nvalid + jnp.arange(N_EXPERTS, dtype=jnp.int32)
    zlo = jnp.concatenate([jnp.where(cnt > 0, (end - MOE_TILE) * nq, -1),
                           jnp.where(tail < NT, tail * (MOE_TILE * nq), -1)]).astype(jnp.int32)
    return (pos.reshape(-1) * nq).astype(jnp.int32), zlo, tile_e, nvalid.reshape(1), n_rows


def kernel(x, c, ctx, c_ctx, w_ada, b_ada, w_in, g_cq, g_ckv, w_uq, w_ukv, g_q, g_k, w_sgu, b_sgu, g_sgu,
           w_pool, s_pool, w_ao, w_bo, w_co, w_out, w_rg, b_rg, w_re, b_re, w_e_gate, w_e_up, w_e_down):
    B, n, D = x.shape
    nc = ctx.shape[1]
    S = n + nc
    T = B * S
    L = w_ada.shape[0]
    perm = _swap_perm()

    R = -(-(B + 1) // SUBLANES) * SUBLANES
    cc = jnp.concatenate([c, c_ctx[None], jnp.zeros((R - B - 1, D), F32)], axis=0)
    mod = _ada(cc, w_ada, b_ada)
    mod2 = mod.reshape(L * R, 1, 6 * D)

    ct, st = _rope_tables(n, S)
    w_in_r = _prep_w_in(w_in, D, perm)
    xs = jnp.concatenate([x, ctx], axis=1)

    for i in range(L):
        base = i * R
        wq = w_uq[i].reshape(Q_LORA, MLA_HEADS, QK_HEAD)
        wq = jnp.concatenate([wq, wq[:, :, QK_NOPE:][:, :, perm]], axis=2).reshape(Q_LORA, -1).astype(_MXU)
        wkv = w_ukv[i].astype(_MXU)
        bias = jnp.repeat(b_sgu[i].T, SGU_GROUP_CH, axis=1)
        w_r = jnp.concatenate([w_rg[i], jnp.moveaxis(w_re[i], 0, 1).reshape(D, N_EXPERTS),
                               jnp.zeros((D, ROUTE_COLS - N_GROUPS - N_EXPERTS), F32)], axis=1)
        b_r = jnp.concatenate([b_rg[i], b_re[i].reshape(-1),
                               jnp.zeros((ROUTE_COLS - N_GROUPS - N_EXPERTS,), F32)])[None]
        w_r_hi = w_r.astype(jnp.bfloat16)
        w_r = jnp.concatenate([w_r_hi, (w_r - w_r_hi.astype(F32)).astype(jnp.bfloat16)], axis=1)

        P = _inproj(xs, mod2, w_in_r, i, base, n)
        q, k, v = _mla_front(P, wq, wkv, g_cq[i][None], g_ckv[i][None],
                             _head_gains(g_q[i], perm), _head_gains(g_k[i], perm), ct, st)
        a_lat, a_ctx = _attention(q, k, v, n)
        bb = _sgu(P, g_sgu[i][None], w_sgu[i].astype(_MXU), bias)
        cp = _pool(P, w_pool[i].astype(_MXU), s_pool[i][None], n)
        xs = _merge(a_lat, a_ctx, bb, cp, P, xs, mod2, w_ao[i].astype(_MXU), w_bo[i].astype(_MXU),
                    w_co[i].astype(_MXU), w_out[i].astype(_MXU), base, n)

        h2, route, counts = _router(xs, mod2, w_r, b_r, base, n)
        pos, zlo, tile_e, nvalid, n_rows = _slots(route.reshape(T, ROUTE_COLS), counts, T, D // (2 * LANES))
        xsorted = _dispatch(pos, zlo, h2, T, n_rows)
        ysorted = _experts(tile_e, nvalid, xsorted, w_e_gate, w_e_up, w_e_down, i)
        xs = _combine(pos, ysorted, xs, route, mod2, base, n, S if i < L - 1 else n)

    return xs
```

```python
import functools
import math

import jax
import jax.numpy as jnp
import numpy as np
from jax import lax
from jax.experimental import pallas as pl
from jax.experimental.pallas import tpu as pltpu

GRID_W = 64
EPS = 1e-6
MLA_HEADS = 8
QK_NOPE = 128
QK_ROPE = 64
V_HEAD = 128
QK_HEAD = QK_NOPE + QK_ROPE
Q_LORA = 512
KV_LORA = 256
MLA_WIDTH = MLA_HEADS * V_HEAD
ROPE_THETA = 10000.0
ROPE_AXIS_DIM = QK_ROPE // 2
ROPE_FREQS = ROPE_AXIS_DIM // 2
ATTN_SCALE = 1.0 / math.sqrt(QK_HEAD)
LOG2E = math.log2(math.e)
SGU_GROUPS = 8
SGU_CHUNK = 128
SGU_GROUP_CH = 128
SGU_WIDTH = SGU_GROUPS * SGU_GROUP_CH
POOL_WINDOWS = (2, 4, 8, 16)
POOL_GROUP_CH = 256
POOL_WIDTH = len(POOL_WINDOWS) * POOL_GROUP_CH
N_GROUPS = 4
EXPERTS_PER_GROUP = 8
N_EXPERTS = N_GROUPS * EXPERTS_PER_GROUP
EXPERT_HIDDEN = 512

LANES = 128
SUBLANES = 8
HEAD_PAD = 256
VMEM_LIMIT = 56 * 1024 * 1024

MLA_BLOCK = 1024
COL_SGU = 0
COL_POOL = 2 * SGU_WIDTH
COL_MLA = COL_POOL + POOL_WIDTH
COL_GATE = COL_MLA + MLA_BLOCK

POOL_HALO = 16
ROUTE_COLS = LANES
MOE_TILE = 512
NORM_CHUNK = 32

_MXU = jnp.bfloat16
_ACT = jnp.bfloat16
F32 = jnp.float32
U32 = jnp.uint32


def _params(*sem):
    return pltpu.CompilerParams(dimension_semantics=sem, vmem_limit_bytes=VMEM_LIMIT)


def _tile(total, cap, mult=LANES):
    best = None
    for t in range(mult, min(total, cap) + 1, mult):
        if total % t == 0:
            best = t
    assert best is not None, (total, cap, mult)
    return best


def _dot(a, b):
    return jnp.dot(a.astype(_MXU), b.astype(_MXU), preferred_element_type=F32)


def _ada_kernel(c_ref, w_ref, b_ref, o_ref):
    c = c_ref[...]
    s = c * jax.nn.sigmoid(c)
    o_ref[0] = _dot(s, w_ref[0]) + b_ref[0]


def _ada(cc, w_ada, b_ada):
    L, D, N = w_ada.shape
    R = cc.shape[0]
    tn = _tile(N, 2048)
    return pl.pallas_call(
        _ada_kernel,
        out_shape=jax.ShapeDtypeStruct((L, R, N), F32),
        grid=(L, N // tn),
        in_specs=[
            pl.BlockSpec((R, D), lambda l, j: (0, 0)),
            pl.BlockSpec((1, D, tn), lambda l, j: (l, 0, j)),
            pl.BlockSpec((1, 1, tn), lambda l, j: (l, 0, j)),
        ],
        out_specs=pl.BlockSpec((1, R, tn), lambda l, j: (l, 0, j)),
        compiler_params=_params("parallel", "parallel"),
        name="ada_mod",
    )(cc, w_ada, b_ada.reshape(L, 1, N))


def _norm_mod(x, row0, n_lat, sh_b, sc_b, sh_c, sc_c):
    ms = jnp.mean(x * x, axis=-1, keepdims=True)
    xn = x * lax.rsqrt(ms + EPS)
    rows = row0 + lax.broadcasted_iota(jnp.int32, (x.shape[0], 1), 0)
    lat = rows < n_lat
    sc = jnp.where(lat, sc_b, sc_c)
    sh = jnp.where(lat, sh_b, sh_c)
    return xn * (1.0 + sc) + sh


def _mod_specs(D, base, B, col):
    return [pl.BlockSpec((1, 1, D), lambda *g: (base + g[0], 0, col)),
            pl.BlockSpec((1, 1, D), lambda *g: (base + B, 0, col))]


def _inproj_kernel(x_ref, shb, shc, scb, scc, w_ref, o_ref, h_ref, *, n_lat, tm):
    @pl.when(pl.program_id(2) == 0)
    def _():
        def chunk(c, carry):
            r0 = pl.multiple_of(c * NORM_CHUNK, NORM_CHUNK)
            h = _norm_mod(x_ref[0, pl.ds(r0, NORM_CHUNK), :], pl.program_id(1) * tm + r0, n_lat,
                          shb[0], scb[0], shc[0], scc[0])
            h_ref[pl.ds(r0, NORM_CHUNK), :] = h.astype(h_ref.dtype)
            return carry
        lax.fori_loop(0, tm // NORM_CHUNK, chunk, 0)
    o_ref[0] = jnp.dot(h_ref[...], w_ref[0], preferred_element_type=F32).astype(o_ref.dtype)


def _inproj(xs, mod, w, layer, base, n_lat):
    B, S, D = xs.shape
    NP = w.shape[2]
    tm = _tile(S, 1024)
    tn = _tile(NP, 2048)
    return pl.pallas_call(
        functools.partial(_inproj_kernel, n_lat=n_lat, tm=tm),
        out_shape=jax.ShapeDtypeStruct((B, S, NP), _ACT),
        grid=(B, S // tm, NP // tn),
        in_specs=[pl.BlockSpec((1, tm, D), lambda b, i, j: (b, i, 0))]
        + _mod_specs(D, base, B, 0) + _mod_specs(D, base, B, 1)
        + [pl.BlockSpec((1, D, tn), lambda b, i, j: (layer, 0, j))],
        out_specs=pl.BlockSpec((1, tm, tn), lambda b, i, j: (b, i, j)),
        scratch_shapes=[pltpu.VMEM((tm, D), _MXU)],
        compiler_params=_params("parallel", "parallel", "arbitrary"),
        name="in_proj",
    )(xs, mod, mod, mod, mod, w)


def _rope_half(b, gct, gst):
    return b * gct + pltpu.roll(b, QK_ROPE, axis=1) * gst


def _sumsq_rope(b):
    lane = lax.broadcasted_iota(jnp.int32, (1, LANES), 1)
    return jnp.sum(jnp.where(lane < QK_ROPE, b * b, 0.0), axis=-1, keepdims=True)


def _mla_kernel(p_ref, wq_ref, wkv_ref, gcq_ref, gckv_ref, gq_ref, gk_ref, ct_ref, st_ref,
                q_ref, k_ref, v_ref):
    p = p_ref[0].astype(F32)
    pq = p[:, :Q_LORA]
    pkv = p[:, Q_LORA:Q_LORA + KV_LORA]
    kr = p[:, MLA_BLOCK - LANES:]
    cq = pq * lax.rsqrt(jnp.mean(pq * pq, axis=-1, keepdims=True) + EPS) * gcq_ref[...]
    ckv = pkv * lax.rsqrt(jnp.mean(pkv * pkv, axis=-1, keepdims=True) + EPS) * gckv_ref[...]
    qf = _dot(cq, wq_ref[...])
    kvf = _dot(ckv, wkv_ref[...])
    ct = ct_ref[...]
    st = st_ref[...]
    gq = gq_ref[...]
    gk = gk_ref[...]
    gq_n, q_gct, q_gst = gq[0:1, :QK_NOPE], gq[0:1, QK_NOPE:] * ct, gq[1:2, QK_NOPE:] * st
    gk_n = gk[0:1, :QK_NOPE]
    k_rot = _rope_half(kr, gk[0:1, QK_NOPE:] * ct, gk[1:2, QK_NOPE:] * st)
    kr_ss = _sumsq_rope(kr)
    for h in range(MLA_HEADS):
        lo = h * HEAD_PAD
        qa, qb = qf[:, lo:lo + QK_NOPE], qf[:, lo + QK_NOPE:lo + HEAD_PAD]
        rq = lax.rsqrt((jnp.sum(qa * qa, axis=-1, keepdims=True) + _sumsq_rope(qb)) * (1.0 / QK_HEAD) + EPS)
        rq = rq * (ATTN_SCALE * LOG2E)
        q_ref[0, :, lo:lo + QK_NOPE] = (qa * gq_n * rq).astype(q_ref.dtype)
        q_ref[0, :, lo + QK_NOPE:lo + HEAD_PAD] = (_rope_half(qb, q_gct, q_gst) * rq).astype(q_ref.dtype)
        ka = kvf[:, lo:lo + QK_NOPE]
        rk = lax.rsqrt((jnp.sum(ka * ka, axis=-1, keepdims=True) + kr_ss) * (1.0 / QK_HEAD) + EPS)
        k_ref[0, :, lo:lo + QK_NOPE] = (ka * gk_n * rk).astype(k_ref.dtype)
        k_ref[0, :, lo + QK_NOPE:lo + HEAD_PAD] = (k_rot * rk).astype(k_ref.dtype)
        v_ref[0, :, h * V_HEAD:(h + 1) * V_HEAD] = kvf[:, lo + QK_NOPE:lo + HEAD_PAD].astype(v_ref.dtype)


def _mla_front(P, wq, wkv, gcq, gckv, gq2, gk2, ct, st):
    B, S, _ = P.shape
    tm = _tile(S, 512)
    HW = MLA_HEADS * HEAD_PAD
    cblk = COL_MLA // MLA_BLOCK
    full = lambda shape: pl.BlockSpec(shape, lambda b, i: (0,) * len(shape))
    return pl.pallas_call(
        _mla_kernel,
        out_shape=(jax.ShapeDtypeStruct((B, S, HW), _ACT),
                   jax.ShapeDtypeStruct((B, S, HW), _ACT),
                   jax.ShapeDtypeStruct((B, S, MLA_WIDTH), _ACT)),
        grid=(B, S // tm),
        in_specs=[pl.BlockSpec((1, tm, MLA_BLOCK), lambda b, i: (b, i, cblk)),
                  full(wq.shape), full(wkv.shape), full(gcq.shape), full(gckv.shape),
                  full(gq2.shape), full(gk2.shape),
                  pl.BlockSpec((tm, LANES), lambda b, i: (i, QK_NOPE // LANES)),
                  pl.BlockSpec((tm, LANES), lambda b, i: (i, QK_NOPE // LANES))],
        out_specs=(pl.BlockSpec((1, tm, HW), lambda b, i: (b, i, 0)),
                   pl.BlockSpec((1, tm, HW), lambda b, i: (b, i, 0)),
                   pl.BlockSpec((1, tm, MLA_WIDTH), lambda b, i: (b, i, 0))),
        compiler_params=_params("parallel", "parallel"),
        name="mla_front",
    )(P, wq, wkv, gcq, gckv, gq2, gk2, ct, st)


def _attn_kernel(q_ref, k_ref, v_ref, o_ref):
    nk = k_ref.shape[1]
    ones = (lax.broadcasted_iota(jnp.int32, (nk, V_HEAD), 1) == 0).astype(v_ref.dtype)
    for h in range(MLA_HEADS):
        q = q_ref[0, :, h * HEAD_PAD:(h + 1) * HEAD_PAD]
        k = k_ref[0, :, h * HEAD_PAD:(h + 1) * HEAD_PAD]
        v = jnp.concatenate([v_ref[0, :, h * V_HEAD:(h + 1) * V_HEAD], ones], axis=1)
        s = lax.dot_general(q, k, (((1,), (1,)), ((), ())), preferred_element_type=F32)
        p = jnp.exp2(s - jnp.max(s, axis=-1, keepdims=True))
        o = jnp.dot(p.astype(v.dtype), v, preferred_element_type=F32)
        o_ref[0, :, h * V_HEAD:(h + 1) * V_HEAD] = (
            o[:, :V_HEAD] / o[:, V_HEAD:V_HEAD + 1]).astype(o_ref.dtype)


def _attention(q, k, v, n_lat):
    B, S, HW = q.shape
    nc = S - n_lat
    assert n_lat % nc == 0, (n_lat, nc)
    tq = _tile(n_lat, 512)
    a_lat = pl.pallas_call(
        _attn_kernel,
        out_shape=jax.ShapeDtypeStruct((B, n_lat, MLA_WIDTH), _ACT),
        grid=(B, n_lat // tq),
        in_specs=[pl.BlockSpec((1, tq, HW), lambda b, i: (b, i, 0)),
                  pl.BlockSpec((1, S, HW), lambda b, i: (b, 0, 0)),
                  pl.BlockSpec((1, S, MLA_WIDTH), lambda b, i: (b, 0, 0))],
        out_specs=pl.BlockSpec((1, tq, MLA_WIDTH), lambda b, i: (b, i, 0)),
        compiler_params=_params("parallel", "arbitrary"),
        name="attention",
    )(q, k, v)
    cblk = n_lat // nc
    ctx_rows = lambda w: pl.BlockSpec((1, nc, w), lambda b: (b, cblk, 0))
    a_ctx = pl.pallas_call(
        _attn_kernel,
        out_shape=jax.ShapeDtypeStruct((B, nc, MLA_WIDTH), _ACT),
        grid=(B,),
        in_specs=[ctx_rows(HW), ctx_rows(HW), ctx_rows(MLA_WIDTH)],
        out_specs=pl.BlockSpec((1, nc, MLA_WIDTH), lambda b: (b, 0, 0)),
        compiler_params=_params("parallel"),
        name="attention_ctx",
    )(q, k, v)
    return a_lat, a_ctx


def _gelu_tanh(x):
    return 0.5 * x * (1.0 + jnp.tanh(math.sqrt(2.0 / math.pi) * (x + 0.044715 * (x * x * x))))


def _sgu_kernel(p_ref, g_ref, w_ref, b_ref, o_ref, *, tm):
    gp = _gelu_tanh(p_ref[0].astype(F32))
    u = gp[:, :SGU_WIDTH]
    v = gp[:, SGU_WIDTH:]
    v = v * lax.rsqrt(jnp.mean(v * v, axis=-1, keepdims=True) + EPS) * g_ref[...]
    vb = v.astype(_MXU)
    bias = b_ref[...]
    for c in range(tm // SGU_CHUNK):
        rs = slice(c * SGU_CHUNK, (c + 1) * SGU_CHUNK)
        for g in range(SGU_GROUPS):
            cs = slice(g * SGU_GROUP_CH, (g + 1) * SGU_GROUP_CH)
            mix = jnp.dot(w_ref[g], vb[rs, cs], preferred_element_type=F32) + bias[:, cs]
            o_ref[0, rs, cs] = (u[rs, cs] * mix).astype(o_ref.dtype)


def _sgu(P, g_sgu, w_sgu, bias, rows):
    B, S, _ = P.shape
    tm = _tile(math.gcd(rows, S), 512)
    full = lambda shape: pl.BlockSpec(shape, lambda b, i: (0,) * len(shape))
    return pl.pallas_call(
        functools.partial(_sgu_kernel, tm=tm),
        out_shape=jax.ShapeDtypeStruct((B, rows, SGU_WIDTH), _ACT),
        grid=(B, rows // tm),
        in_specs=[pl.BlockSpec((1, tm, 2 * SGU_WIDTH), lambda b, i: (b, i, COL_SGU // (2 * SGU_WIDTH))),
                  full(g_sgu.shape), full(w_sgu.shape), full(bias.shape)],
        out_specs=pl.BlockSpec((1, tm, SGU_WIDTH), lambda b, i: (b, i, 0)),
        compiler_params=_params("parallel", "parallel"),
        name="spatial_gating",
    )(P, g_sgu, w_sgu, bias)


def _window_sum(x, g):
    n, C = x.shape
    z = jnp.zeros((POOL_HALO, C), F32)
    xe = jnp.concatenate([z, x, z], axis=0)
    ne = n + 2 * POOL_HALO
    w = xe + pltpu.roll(xe, 1, axis=0)
    for s in range(g):
        k = 1 << s
        w = pltpu.roll(w, k, axis=0) + pltpu.roll(w, ne - k, axis=0)
    return w[POOL_HALO:POOL_HALO + n]


def _pool_kernel(p_ref, w_ref, s_ref, o_ref, *, n_lat):
    rows = o_ref.shape[1]
    for lo, n in ((0, n_lat), (n_lat, rows - n_lat)):
        if n == 0:
            continue
        t = lax.broadcasted_iota(jnp.int32, (n, 1), 0)
        for g, win in enumerate(POOL_WINDOWS):
            cs = slice(g * POOL_GROUP_CH, (g + 1) * POOL_GROUP_CH)
            x = p_ref[0, lo:lo + n, cs].astype(F32)
            half = win // 2
            cnt = (jnp.minimum(t + half, n) - jnp.maximum(t - half, 0)).astype(F32)
            m = _window_sum(x, g) / cnt - x
            y = _dot(m, w_ref[g]) * s_ref[:, cs]
            o_ref[0, lo:lo + n, cs] = y.astype(o_ref.dtype)


def _pool(P, w_pool, s_pool, n_lat, rows):
    B, S, _ = P.shape
    full = lambda shape: pl.BlockSpec(shape, lambda b: (0,) * len(shape))
    return pl.pallas_call(
        functools.partial(_pool_kernel, n_lat=n_lat),
        out_shape=jax.ShapeDtypeStruct((B, rows, POOL_WIDTH), _ACT),
        grid=(B,),
        in_specs=[pl.BlockSpec((1, S, POOL_WIDTH), lambda b: (b, 0, COL_POOL // POOL_WIDTH)),
                  full(w_pool.shape), full(s_pool.shape)],
        out_specs=pl.BlockSpec((1, rows, POOL_WIDTH), lambda b: (b, 0, 0)),
        compiler_params=_params("parallel"),
        name="multiscale_pool",
    )(P, w_pool, s_pool)


def _merge_kernel(al_ref, ac_ref, b_ref, c_ref, ga_ref, gb_ref, gc_ref, x_ref, g1b, g1c,
                  wa_ref, wb_ref, wc_ref, wo_ref, o_ref, *, n_lat, tm):
    def gate(ref):
        return jax.nn.sigmoid(ref[0].astype(F32))
    a = jnp.where(pl.program_id(1) * tm < n_lat, al_ref[0], ac_ref[0])
    y = gate(ga_ref) * jnp.dot(a, wa_ref[...], preferred_element_type=F32)
    y = y + gate(gb_ref) * jnp.dot(b_ref[0], wb_ref[...], preferred_element_type=F32)
    y = y + gate(gc_ref) * jnp.dot(c_ref[0], wc_ref[...], preferred_element_type=F32)
    out = _dot(y, wo_ref[...])
    rows = pl.program_id(1) * tm + lax.broadcasted_iota(jnp.int32, (tm, 1), 0)
    g1 = jnp.where(rows < n_lat, g1b[0], g1c[0])
    o_ref[0] = x_ref[0] + g1 * out


def _merge(a_lat, a_ctx, b, cp, P, xs, mod, wa, wb, wc, wo, base, n_lat, rows):
    B, S, D = xs.shape
    tm = _tile(math.gcd(n_lat, S - n_lat), 256)
    gblk = COL_GATE // D
    nl = n_lat // tm
    tok = lambda w: pl.BlockSpec((1, tm, w), lambda bb, i: (bb, i, 0))
    att_lat = pl.BlockSpec((1, tm, MLA_WIDTH), lambda bb, i: (bb, jnp.minimum(i, nl - 1), 0))
    att_ctx = pl.BlockSpec((1, tm, MLA_WIDTH), lambda bb, i: (bb, jnp.maximum(i - nl, 0), 0))
    gate = lambda k: pl.BlockSpec((1, tm, D), lambda bb, i: (bb, i, gblk + k))
    wres = lambda shape: pl.BlockSpec(shape, lambda bb, i: (0, 0), pipeline_mode=pl.Buffered(1))
    return pl.pallas_call(
        functools.partial(_merge_kernel, n_lat=n_lat, tm=tm),
        out_shape=jax.ShapeDtypeStruct((B, rows, D), F32),
        grid=(B, rows // tm),
        in_specs=[att_lat, att_ctx, tok(SGU_WIDTH), tok(POOL_WIDTH), gate(0), gate(1), gate(2), tok(D)]
        + _mod_specs(D, base, B, 2)
        + [wres(wa.shape), wres(wb.shape), wres(wc.shape), wres(wo.shape)],
        out_specs=tok(D),
        compiler_params=_params("parallel", "parallel"),
        name="merge_out",
    )(a_lat, a_ctx, b, cp, P, P, P, xs, mod, mod, wa, wb, wc, wo)


def _pack_rows(o_ref, y):
    rows, half = y.shape[0], y.shape[1] // 2
    nq = half // LANES
    lo = lax.bitcast_convert_type(y[:, :half].astype(jnp.bfloat16).astype(F32), U32)
    hi = lax.bitcast_convert_type(y[:, half:].astype(jnp.bfloat16).astype(F32), U32)
    w = hi | (lo >> 16)
    for q in range(nq):
        o_ref[pl.ds(q, rows, stride=nq), :] = w[:, q * LANES:(q + 1) * LANES]


def _unpack_rows(ref, q, rows, nq):
    w = ref[pl.ds(q, rows, stride=nq), :]
    lo = lax.bitcast_convert_type(w << 16, F32)
    hi = lax.bitcast_convert_type(w & jnp.uint32(0xFFFF0000), F32)
    return lo, hi


def _router_kernel(x_ref, shb, shc, scb, scc, w_ref, b_ref, h_ref, r_ref, cnt_ref, *, n_lat, tm):
    first = (pl.program_id(0) == 0) & (pl.program_id(1) == 0)

    @pl.when(first)
    def _():
        cnt_ref[...] = jnp.zeros_like(cnt_ref)

    h = _norm_mod(x_ref[0], pl.program_id(1) * tm, n_lat, shb[0], scb[0], shc[0], scc[0])
    _pack_rows(h_ref, h)
    h_hi = h.astype(jnp.bfloat16)
    h_lo = (h - h_hi.astype(F32)).astype(jnp.bfloat16)
    t = jnp.dot(h_hi, w_ref[...], preferred_element_type=F32)
    logits = (t[:, :ROUTE_COLS] + t[:, ROUTE_COLS:]
              + jnp.dot(h_lo, w_ref[:, :ROUTE_COLS], preferred_element_type=F32) + b_ref[...])
    lane = lax.broadcasted_iota(jnp.int32, (tm, ROUTE_COLS), 1)
    ninf = -jnp.inf
    lg = jnp.where(lane < N_GROUPS, logits, ninf)
    mg = jnp.max(lg, axis=-1, keepdims=True)
    g_w = 1.0 / jnp.sum(jnp.exp(lg - mg), axis=-1, keepdims=True)
    g_i = jnp.min(jnp.where(lg == mg, lane, ROUTE_COLS), axis=-1, keepdims=True)
    lo = N_GROUPS + g_i * EXPERTS_PER_GROUP
    le = jnp.where((lane >= lo) & (lane < lo + EXPERTS_PER_GROUP), logits, ninf)
    m1 = jnp.max(le, axis=-1, keepdims=True)
    i1 = jnp.min(jnp.where(le == m1, lane, ROUTE_COLS), axis=-1, keepdims=True)
    le2 = jnp.where(lane == i1, ninf, le)
    m2 = jnp.max(le2, axis=-1, keepdims=True)
    i2 = jnp.min(jnp.where(le2 == m2, lane, ROUTE_COLS), axis=-1, keepdims=True)
    e21 = jnp.exp(m2 - m1)
    w1 = 1.0 / (1.0 + e21)
    w2 = e21 * w1
    onehot = jnp.where((lane == i1) | (lane == i2), 1.0, 0.0)
    ti = lax.broadcasted_iota(jnp.int32, (tm, tm), 0)
    tj = lax.broadcasted_iota(jnp.int32, (tm, tm), 1)
    tri = jnp.where(ti > tj, 1.0, 0.0).astype(jnp.bfloat16)
    before = jnp.dot(tri, onehot.astype(jnp.bfloat16), preferred_element_type=F32) + cnt_ref[...]
    r1 = jnp.sum(jnp.where(lane == i1, before, 0.0), axis=-1, keepdims=True)
    r2 = jnp.sum(jnp.where(lane == i2, before, 0.0), axis=-1, keepdims=True)
    cnt_ref[...] += jnp.sum(onehot, axis=0, keepdims=True)
    out = jnp.where(lane == 0, (i1 - N_GROUPS).astype(F32), 0.0)
    out = jnp.where(lane == 1, (i2 - N_GROUPS).astype(F32), out)
    out = jnp.where(lane == 2, g_w * w1, out)
    out = jnp.where(lane == 3, g_w * w2, out)
    out = jnp.where(lane == 4, r1, out)
    out = jnp.where(lane == 5, r2, out)
    r_ref[0] = out


def _router(xs, mod, w_r, b_r, base, n_lat):
    B, S, D = xs.shape
    tm = _tile(S, 512)
    Q2 = D // (2 * LANES)
    nt = S // tm
    full = lambda shape: pl.BlockSpec(shape, lambda b, i: (0,) * len(shape))
    return pl.pallas_call(
        functools.partial(_router_kernel, n_lat=n_lat, tm=tm),
        out_shape=(jax.ShapeDtypeStruct((B * S * Q2, LANES), U32),
                   jax.ShapeDtypeStruct((B, S, ROUTE_COLS), F32),
                   jax.ShapeDtypeStruct((1, ROUTE_COLS), F32)),
        grid=(B, S // tm),
        in_specs=[pl.BlockSpec((1, tm, D), lambda b, i: (b, i, 0))]
        + _mod_specs(D, base, B, 3) + _mod_specs(D, base, B, 4)
        + [full(w_r.shape), full(b_r.shape)],
        out_specs=(pl.BlockSpec((tm * Q2, LANES), lambda b, i: (b * nt + i, 0)),
                   pl.BlockSpec((1, tm, ROUTE_COLS), lambda b, i: (b, i, 0)),
                   pl.BlockSpec((1, ROUTE_COLS), lambda b, i: (0, 0))),
        compiler_params=_params("arbitrary", "arbitrary"),
        name="moe_router",
    )(xs, mod, mod, mod, mod, w_r, b_r)


def _dispatch_kernel(pos_ref, zlo_ref, h_ref, o_ref, zero_ref, stage, ld_sem, sc_sem, zsem, *, td, nq):
    c = pl.program_id(0)
    n = pl.num_programs(0)
    rows = td * nq
    slot = c % DISPATCH_SLOTS

    def load(chunk, s):
        src = h_ref.at[pl.ds(pl.multiple_of(chunk * rows, rows), rows)]
        return pltpu.make_async_copy(src, stage.at[s], ld_sem.at[s])

    def row_copies_of(s):
        return pltpu.make_async_copy(h_ref.at[pl.ds(0, 2 * rows)], o_ref.at[pl.ds(0, 2 * rows)], sc_sem.at[s])

    def zero_fill(e):
        start = pl.multiple_of(zlo_ref[e], nq)
        return pltpu.make_async_copy(zero_ref, o_ref.at[pl.ds(start, MOE_TILE * nq)], zsem)

    @pl.when(c == 0)
    def _():
        zero_ref[...] = jnp.zeros_like(zero_ref)
        for e in range(2 * N_EXPERTS):
            @pl.when(zlo_ref[e] >= 0)
            def _():
                zero_fill(e).start()
        for e in range(2 * N_EXPERTS):
            @pl.when(zlo_ref[e] >= 0)
            def _():
                zero_fill(e).wait()

        load(0, 0).start()

    @pl.when(c >= 2)
    def _():
        row_copies_of((c + 1) % DISPATCH_SLOTS).wait()

    @pl.when(c + 1 < n)
    def _():
        load(c + 1, (c + 1) % DISPATCH_SLOTS).start()

    load(c, slot).wait()

    def issue(r, carry):
        src = stage.at[slot, pl.ds(pl.multiple_of(r * nq, nq), nq)]
        for k in range(2):
            dst = o_ref.at[pl.ds(pl.multiple_of(pos_ref[2 * (c * td + r) + k], nq), nq)]
            pltpu.make_async_copy(src, dst, sc_sem.at[slot]).start(priority=k)
        return carry

    lax.fori_loop(0, td, issue, 0)

    @pl.when(c == n - 1)
    def _():
        @pl.when(c >= 1)
        def _():
            row_copies_of((c + DISPATCH_SLOTS - 1) % DISPATCH_SLOTS).wait()
        row_copies_of(slot).wait()


DISPATCH_SLOTS = 3


def _dispatch(pos, zlo, h2p, T, n_rows):
    nq = h2p.shape[0] // T
    td = _tile(T // 2, 512, SUBLANES)
    return pl.pallas_call(
        functools.partial(_dispatch_kernel, td=td, nq=nq),
        out_shape=jax.ShapeDtypeStruct((n_rows * nq, LANES), h2p.dtype),
        grid_spec=pltpu.PrefetchScalarGridSpec(
            num_scalar_prefetch=2,
            grid=(T // td,),
            in_specs=[pl.BlockSpec(memory_space=pl.ANY)],
            out_specs=pl.BlockSpec(memory_space=pl.ANY),
            scratch_shapes=[pltpu.VMEM((MOE_TILE * nq, LANES), h2p.dtype),
                            pltpu.VMEM((DISPATCH_SLOTS, td * nq, LANES), h2p.dtype),
                            pltpu.SemaphoreType.DMA((DISPATCH_SLOTS,)),
                            pltpu.SemaphoreType.DMA((DISPATCH_SLOTS,)),
                            pltpu.SemaphoreType.DMA(())],
        ),
        compiler_params=pltpu.CompilerParams(dimension_semantics=("arbitrary",),
                                             vmem_limit_bytes=VMEM_LIMIT, has_side_effects=True),
        name="moe_dispatch",
    )(pos, zlo, h2p)


def _expert_kernel(te_ref, nv_ref, x_ref, wg_ref, wu_ref, wd_ref, o_ref, wgu_s, wd_s, *, nq):
    j = pl.program_id(0)
    used = j < nv_ref[0]
    new_expert = (j == 0) | (te_ref[j] != te_ref[jnp.maximum(j - 1, 0)])

    @pl.when(used & new_expert)
    def _():
        wgu_s[:, :EXPERT_HIDDEN] = wg_ref[0, 0].astype(wgu_s.dtype)
        wgu_s[:, EXPERT_HIDDEN:] = wu_ref[0, 0].astype(wgu_s.dtype)
        wd_s[...] = wd_ref[0, 0].astype(wd_s.dtype)

    @pl.when(used)
    def _():
        los, his = [], []
        for q in range(nq):
            lo, hi = _unpack_rows(x_ref, q, MOE_TILE, nq)
            los.append(lo.astype(_MXU))
            his.append(hi.astype(_MXU))
        x = jnp.concatenate(los + his, axis=1)
        hgu = jnp.dot(x, wgu_s[...], preferred_element_type=F32)
        hg = hgu[:, :EXPERT_HIDDEN]
        hid = hg * jax.nn.sigmoid(hg) * hgu[:, EXPERT_HIDDEN:]
        _pack_rows(o_ref, _dot(hid, wd_s[...]))

    @pl.when(jnp.logical_not(used))
    def _():
        o_ref[...] = jnp.zeros_like(o_ref)


def _experts(tile_e, nvalid, xsorted, wg, wu, wd, layer):
    _, E, D, Hd = wg.shape
    nq = D // (2 * LANES)
    NT = xsorted.shape[0] // (MOE_TILE * nq)
    return pl.pallas_call(
        functools.partial(_expert_kernel, nq=nq),
        out_shape=jax.ShapeDtypeStruct(xsorted.shape, U32),
        grid_spec=pltpu.PrefetchScalarGridSpec(
            num_scalar_prefetch=2,
            grid=(NT,),
            in_specs=[pl.BlockSpec((MOE_TILE * nq, LANES), lambda j, te, nv: (j, 0)),
                      pl.BlockSpec((1, 1, D, Hd), lambda j, te, nv: (layer, te[j], 0, 0)),
                      pl.BlockSpec((1, 1, D, Hd), lambda j, te, nv: (layer, te[j], 0, 0)),
                      pl.BlockSpec((1, 1, Hd, D), lambda j, te, nv: (layer, te[j], 0, 0))],
            out_specs=pl.BlockSpec((MOE_TILE * nq, LANES), lambda j, te, nv: (j, 0)),
            scratch_shapes=[pltpu.VMEM((D, 2 * Hd), _MXU), pltpu.VMEM((Hd, D), _MXU)],
        ),
        compiler_params=_params("arbitrary"),
        name="moe_experts",
    )(tile_e, nvalid, xsorted, wg, wu, wd)


def _combine_kernel(pos_ref, y_ref, x_ref, r_ref, g2b, g2c, o_ref, a0, a1, b0, b1, sem_a, sem_b,
                    *, n_lat, tc, S, nq):
    b = pl.program_id(0)
    i = pl.program_id(1)
    nt = pl.num_programs(1)
    step = b * nt + i
    last = pl.num_programs(0) * nt - 1
    sets = ((a0, a1, sem_a), (b0, b1, sem_b))

    def start_gathers(base, bufs):
        buf0, buf1, sem = bufs

        def issue(r, carry):
            dst = pl.ds(pl.multiple_of(r * nq, nq), nq)
            for k, buf in ((0, buf0), (1, buf1)):
                src = y_ref.at[pl.ds(pl.multiple_of(pos_ref[2 * (base + r) + k], nq), nq)]
                pltpu.make_async_copy(src, buf.at[dst], sem).start(priority=k)
            return carry

        lax.fori_loop(0, tc, issue, 0)

    def consume(bufs):
        buf0, buf1, sem = bufs
        for buf in (buf0, buf1):
            pltpu.make_async_copy(y_ref.at[pl.ds(0, tc * nq)], buf, sem).wait()
        r = r_ref[0]
        c1, c2 = r[:, 2:3], r[:, 3:4]
        rows = i * tc + lax.broadcasted_iota(jnp.int32, (tc, 1), 0)
        lat = rows < n_lat
        half = x_ref.shape[2] // 2
        for q in range(nq):
            lo0, hi0 = _unpack_rows(buf0, q, tc, nq)
            lo1, hi1 = _unpack_rows(buf1, q, tc, nq)
            for col, y0, y1 in ((q * LANES, lo0, lo1), (half + q * LANES, hi0, hi1)):
                cs = slice(col, col + LANES)
                g2 = jnp.where(lat, g2b[0, :, cs], g2c[0, :, cs])
                o_ref[0, :, cs] = x_ref[0, :, cs] + g2 * (c1 * y0 + c2 * y1)

    @pl.when(step == 0)
    def _():
        start_gathers(0, sets[0])

    next_base = jnp.where(i + 1 < nt, b * S + (i + 1) * tc, (b + 1) * S)
    for parity in range(2):
        @pl.when((step < last) & (step % 2 == parity))
        def _():
            start_gathers(next_base, sets[1 - parity])

    for parity in range(2):
        @pl.when(step % 2 == parity)
        def _():
            consume(sets[parity])


def _combine(pos, ysorted, xs, route, mod, base, n_lat, rows):
    B, S, D = xs.shape
    nq = D // (2 * LANES)
    tc = _tile(math.gcd(rows, S), 256, SUBLANES)
    return pl.pallas_call(
        functools.partial(_combine_kernel, n_lat=n_lat, tc=tc, S=S, nq=nq),
        out_shape=jax.ShapeDtypeStruct((B, rows, D), F32),
        grid_spec=pltpu.PrefetchScalarGridSpec(
            num_scalar_prefetch=1,
            grid=(B, rows // tc),
            in_specs=[pl.BlockSpec(memory_space=pl.ANY),
                      pl.BlockSpec((1, tc, D), lambda b, i, pos: (b, i, 0)),
                      pl.BlockSpec((1, tc, ROUTE_COLS), lambda b, i, pos: (b, i, 0))]
            + _mod_specs(D, base, B, 5),
            out_specs=pl.BlockSpec((1, tc, D), lambda b, i, pos: (b, i, 0)),
            scratch_shapes=[pltpu.VMEM((tc * nq, LANES), U32) for _ in range(4)]
            + [pltpu.SemaphoreType.DMA(()), pltpu.SemaphoreType.DMA(())],
        ),
        compiler_params=_params("arbitrary", "arbitrary"),
        name="moe_combine",
    )(pos, ysorted, xs, route, mod, mod)


def _swap_perm():
    j = np.arange(QK_ROPE)
    a, h, f = j // ROPE_AXIS_DIM, (j // ROPE_FREQS) % 2, j % ROPE_FREQS
    return a * ROPE_AXIS_DIM + (1 - h) * ROPE_FREQS + f


def _rope_tables(n_lat, S):
    t = jnp.arange(n_lat)
    row = (t // GRID_W).astype(F32)
    col = (t % GRID_W).astype(F32)
    inv = ROPE_THETA ** (-jnp.arange(0, ROPE_AXIS_DIM, 2, dtype=F32) / ROPE_AXIS_DIM)
    a0, a1 = row[:, None] * inv, col[:, None] * inv
    c64 = jnp.concatenate([jnp.cos(a0), jnp.cos(a0), jnp.cos(a1), jnp.cos(a1)], axis=1)
    s64 = jnp.concatenate([-jnp.sin(a0), jnp.sin(a0), -jnp.sin(a1), jnp.sin(a1)], axis=1)
    nc = S - n_lat
    c64 = jnp.concatenate([c64, jnp.ones((nc, QK_ROPE), F32)], axis=0)
    s64 = jnp.concatenate([s64, jnp.zeros((nc, QK_ROPE), F32)], axis=0)
    ct = jnp.concatenate([jnp.ones((S, QK_NOPE), F32), c64, jnp.zeros((S, QK_ROPE), F32)], axis=1)
    st = jnp.concatenate([jnp.zeros((S, QK_NOPE), F32), s64, jnp.zeros((S, QK_ROPE), F32)], axis=1)
    return ct, st


def _head_gains(g, perm):
    z = jnp.zeros((QK_ROPE,), F32)
    ga = jnp.concatenate([g[:QK_NOPE], g[QK_NOPE:], z])
    gb = jnp.concatenate([jnp.zeros((QK_NOPE,), F32), g[QK_NOPE:][perm], z])
    return jnp.stack([ga, gb], axis=0)


def _prep_w_in(w, D, perm):
    o = np.cumsum((0, Q_LORA, KV_LORA, QK_ROPE, 2 * SGU_WIDTH, POOL_WIDTH, 3 * D))
    wq, wkv, wkr, wsgu, wpool, wgate = (w[:, :, o[k]:o[k + 1]].astype(_MXU) for k in range(6))
    pad = jnp.zeros(w.shape[:2] + (MLA_BLOCK - Q_LORA - KV_LORA - 2 * QK_ROPE,), _MXU)
    return jnp.concatenate([wsgu, wpool, wq, wkv, pad, wkr, wkr[:, :, perm], wgate], axis=2)


def _slots(route, counts, T, nq):
    e = route[:, 0:2].astype(jnp.int32)
    rank = route[:, 4:6].astype(jnp.int32)
    cnt = counts[0, N_GROUPS:N_GROUPS + N_EXPERTS].astype(jnp.int32)
    padded = (cnt + MOE_TILE - 1) // MOE_TILE * MOE_TILE
    end = jnp.cumsum(padded)
    off = end - padded
    own = e[:, :, None] == jnp.arange(N_EXPERTS, dtype=jnp.int32)
    pos = rank + jnp.sum(jnp.where(own, off, 0), axis=-1)
    n_rows = 2 * T + N_EXPERTS * MOE_TILE
    NT = n_rows // MOE_TILE
    nvalid = (end[-1] // MOE_TILE).astype(jnp.int32)
    tiles = jnp.minimum(jnp.arange(NT, dtype=jnp.int32), nvalid - 1)
    tile_e = jnp.sum((end[None, :] <= (tiles * MOE_TILE)[:, None]).astype(jnp.int32), axis=1)
    tail = nvalid + jnp.arange(N_EXPERTS, dtype=jnp.int32)
    zlo = jnp.concatenate([jnp.where(cnt > 0, (end - MOE_TILE) * nq, -1),
                           jnp.where(tail < NT, tail * (MOE_TILE * nq), -1)]).astype(jnp.int32)
    return (pos.reshape(-1) * nq).astype(jnp.int32), zlo, tile_e, nvalid.reshape(1), n_rows


def kernel(x, c, ctx, c_ctx, w_ada, b_ada, w_in, g_cq, g_ckv, w_uq, w_ukv, g_q, g_k, w_sgu, b_sgu, g_sgu,
           w_pool, s_pool, w_ao, w_bo, w_co, w_out, w_rg, b_rg, w_re, b_re, w_e_gate, w_e_up, w_e_down):
    B, n, D = x.shape
    nc = ctx.shape[1]
    S = n + nc
    T = B * S
    L = w_ada.shape[0]
    perm = _swap_perm()

    R = -(-(B + 1) // SUBLANES) * SUBLANES
    cc = jnp.concatenate([c, c_ctx[None], jnp.zeros((R - B - 1, D), F32)], axis=0)
    mod = _ada(cc, w_ada, b_ada)
    mod2 = mod.reshape(L * R, 1, 6 * D)

    ct, st = _rope_tables(n, S)
    w_in_r = _prep_w_in(w_in, D, perm)
    xs = jnp.concatenate([x, ctx], axis=1)

    for i in range(L):
        base = i * R
        wq = w_uq[i].reshape(Q_LORA, MLA_HEADS, QK_HEAD)
        wq = jnp.concatenate([wq, wq[:, :, QK_NOPE:][:, :, perm]], axis=2).reshape(Q_LORA, -1).astype(_MXU)
        wkv = w_ukv[i].astype(_MXU)
        bias = jnp.repeat(b_sgu[i].T, SGU_GROUP_CH, axis=1)
        w_r = jnp.concatenate([w_rg[i], jnp.moveaxis(w_re[i], 0, 1).reshape(D, N_EXPERTS),
                               jnp.zeros((D, ROUTE_COLS - N_GROUPS - N_EXPERTS), F32)], axis=1)
        b_r = jnp.concatenate([b_rg[i], b_re[i].reshape(-1),
                               jnp.zeros((ROUTE_COLS - N_GROUPS - N_EXPERTS,), F32)])[None]
        w_r_hi = w_r.astype(jnp.bfloat16)
        w_r = jnp.concatenate([w_r_hi, (w_r - w_r_hi.astype(F32)).astype(jnp.bfloat16)], axis=1)

        P = _inproj(xs, mod2, w_in_r, i, base, n)
        q, k, v = _mla_front(P, wq, wkv, g_cq[i][None], g_ckv[i][None],
                             _head_gains(g_q[i], perm), _head_gains(g_k[i], perm), ct, st)
        a_lat, a_ctx = _attention(q, k, v, n)
        rows = S if i < L - 1 else n
        bb = _sgu(P, g_sgu[i][None], w_sgu[i].astype(_MXU), bias, rows)
        cp = _pool(P, w_pool[i].astype(_MXU), s_pool[i][None], n, rows)
        xs = _merge(a_lat, a_ctx, bb, cp, P, xs, mod2, w_ao[i].astype(_MXU), w_bo[i].astype(_MXU),
                    w_co[i].astype(_MXU), w_out[i].astype(_MXU), base, n, rows)

        Tm = B * rows
        h2, route, counts = _router(xs, mod2, w_r, b_r, base, n)
        pos, zlo, tile_e, nvalid, n_rows = _slots(route.reshape(Tm, ROUTE_COLS), counts, Tm, D // (2 * LANES))
        xsorted = _dispatch(pos, zlo, h2, Tm, n_rows)
        ysorted = _experts(tile_e, nvalid, xsorted, w_e_gate, w_e_up, w_e_down, i)
        xs = _combine(pos, ysorted, xs, route, mod2, base, n, rows)

    return xs
```

```python
import functools
import math

import jax
import jax.numpy as jnp
import numpy as np
from jax import lax
from jax.experimental import pallas as pl
from jax.experimental.pallas import tpu as pltpu

GRID_W = 64
EPS = 1e-6
MLA_HEADS = 8
QK_NOPE = 128
QK_ROPE = 64
V_HEAD = 128
QK_HEAD = QK_NOPE + QK_ROPE
Q_LORA = 512
KV_LORA = 256
MLA_WIDTH = MLA_HEADS * V_HEAD
ROPE_THETA = 10000.0
ROPE_AXIS_DIM = QK_ROPE // 2
ROPE_FREQS = ROPE_AXIS_DIM // 2
ATTN_SCALE = 1.0 / math.sqrt(QK_HEAD)
LOG2E = math.log2(math.e)
SGU_GROUPS = 8
SGU_CHUNK = 128
SGU_GROUP_CH = 128
SGU_WIDTH = SGU_GROUPS * SGU_GROUP_CH
POOL_WINDOWS = (2, 4, 8, 16)
POOL_GROUP_CH = 256
POOL_WIDTH = len(POOL_WINDOWS) * POOL_GROUP_CH
N_GROUPS = 4
EXPERTS_PER_GROUP = 8
N_EXPERTS = N_GROUPS * EXPERTS_PER_GROUP
EXPERT_HIDDEN = 512

LANES = 128
SUBLANES = 8
HEAD_PAD = 256
VMEM_LIMIT = 56 * 1024 * 1024

MLA_BLOCK = 1024
COL_SGU = 0
COL_POOL = 2 * SGU_WIDTH
COL_MLA = COL_POOL + POOL_WIDTH
COL_GATE = COL_MLA + MLA_BLOCK

POOL_HALO = 16
ROUTE_COLS = LANES
MOE_TILE = 512
NORM_CHUNK = 32

_MXU = jnp.bfloat16
_ACT = jnp.bfloat16
F32 = jnp.float32
U32 = jnp.uint32


def _params(*sem):
    return pltpu.CompilerParams(dimension_semantics=sem, vmem_limit_bytes=VMEM_LIMIT)


def _tile(total, cap, mult=LANES):
    best = None
    for t in range(mult, min(total, cap) + 1, mult):
        if total % t == 0:
            best = t
    assert best is not None, (total, cap, mult)
    return best


def _dot(a, b):
    return jnp.dot(a.astype(_MXU), b.astype(_MXU), preferred_element_type=F32)


def _ada_kernel(c_ref, w_ref, b_ref, o_ref):
    c = c_ref[...]
    s = c * jax.nn.sigmoid(c)
    o_ref[0] = _dot(s, w_ref[0]) + b_ref[0]


def _ada(cc, w_ada, b_ada):
    L, D, N = w_ada.shape
    R = cc.shape[0]
    tn = _tile(N, 2048)
    return pl.pallas_call(
        _ada_kernel,
        out_shape=jax.ShapeDtypeStruct((L, R, N), F32),
        grid=(L, N // tn),
        in_specs=[
            pl.BlockSpec((R, D), lambda l, j: (0, 0)),
            pl.BlockSpec((1, D, tn), lambda l, j: (l, 0, j)),
            pl.BlockSpec((1, 1, tn), lambda l, j: (l, 0, j)),
        ],
        out_specs=pl.BlockSpec((1, R, tn), lambda l, j: (l, 0, j)),
        compiler_params=_params("parallel", "parallel"),
        name="ada_mod",
    )(cc, w_ada, b_ada.reshape(L, 1, N))


def _norm_mod(x, row0, n_lat, sh_b, sc_b, sh_c, sc_c):
    ms = jnp.mean(x * x, axis=-1, keepdims=True)
    xn = x * lax.rsqrt(ms + EPS)
    rows = row0 + lax.broadcasted_iota(jnp.int32, (x.shape[0], 1), 0)
    lat = rows < n_lat
    sc = jnp.where(lat, sc_b, sc_c)
    sh = jnp.where(lat, sh_b, sh_c)
    return xn * (1.0 + sc) + sh


def _mod_specs(D, base, B, col):
    return [pl.BlockSpec((1, 1, D), lambda *g: (base + g[0], 0, col)),
            pl.BlockSpec((1, 1, D), lambda *g: (base + B, 0, col))]


def _inproj_kernel(x_ref, shb, shc, scb, scc, w_ref, o_ref, h_ref, *, n_lat, tm):
    @pl.when(pl.program_id(2) == 0)
    def _():
        def chunk(c, carry):
            r0 = pl.multiple_of(c * NORM_CHUNK, NORM_CHUNK)
            h = _norm_mod(x_ref[0, pl.ds(r0, NORM_CHUNK), :], pl.program_id(1) * tm + r0, n_lat,
                          shb[0], scb[0], shc[0], scc[0])
            h_ref[pl.ds(r0, NORM_CHUNK), :] = h.astype(h_ref.dtype)
            return carry
        lax.fori_loop(0, tm // NORM_CHUNK, chunk, 0)
    o_ref[0] = jnp.dot(h_ref[...], w_ref[0], preferred_element_type=F32).astype(o_ref.dtype)


def _inproj(xs, mod, w, layer, base, n_lat):
    B, S, D = xs.shape
    NP = w.shape[2]
    tm = _tile(S, 1152)
    tn = _tile(NP, 2048)
    return pl.pallas_call(
        functools.partial(_inproj_kernel, n_lat=n_lat, tm=tm),
        out_shape=jax.ShapeDtypeStruct((B, S, NP), _ACT),
        grid=(B, S // tm, NP // tn),
        in_specs=[pl.BlockSpec((1, tm, D), lambda b, i, j: (b, i, 0))]
        + _mod_specs(D, base, B, 0) + _mod_specs(D, base, B, 1)
        + [pl.BlockSpec((1, D, tn), lambda b, i, j: (layer, 0, j))],
        out_specs=pl.BlockSpec((1, tm, tn), lambda b, i, j: (b, i, j)),
        scratch_shapes=[pltpu.VMEM((tm, D), _MXU)],
        compiler_params=_params("parallel", "parallel", "arbitrary"),
        name="in_proj",
    )(xs, mod, mod, mod, mod, w)


def _rope_half(b, gct, gst):
    return b * gct + pltpu.roll(b, QK_ROPE, axis=1) * gst


def _sumsq_rope(b):
    lane = lax.broadcasted_iota(jnp.int32, (1, LANES), 1)
    return jnp.sum(jnp.where(lane < QK_ROPE, b * b, 0.0), axis=-1, keepdims=True)


def _mla_kernel(p_ref, wq_ref, wkv_ref, gcq_ref, gckv_ref, gq_ref, gk_ref, ct_ref, st_ref,
                q_ref, k_ref, v_ref):
    p = p_ref[0].astype(F32)
    pq = p[:, :Q_LORA]
    pkv = p[:, Q_LORA:Q_LORA + KV_LORA]
    kr = p[:, MLA_BLOCK - LANES:]
    cq = pq * lax.rsqrt(jnp.mean(pq * pq, axis=-1, keepdims=True) + EPS) * gcq_ref[...]
    ckv = pkv * lax.rsqrt(jnp.mean(pkv * pkv, axis=-1, keepdims=True) + EPS) * gckv_ref[...]
    qf = _dot(cq, wq_ref[...])
    kvf = _dot(ckv, wkv_ref[...])
    ct = ct_ref[...]
    st = st_ref[...]
    gq = gq_ref[...]
    gk = gk_ref[...]
    gq_n, q_gct, q_gst = gq[0:1, :QK_NOPE], gq[0:1, QK_NOPE:] * ct, gq[1:2, QK_NOPE:] * st
    gk_n = gk[0:1, :QK_NOPE]
    k_rot = _rope_half(kr, gk[0:1, QK_NOPE:] * ct, gk[1:2, QK_NOPE:] * st)
    kr_ss = _sumsq_rope(kr)
    for h in range(MLA_HEADS):
        lo = h * HEAD_PAD
        qa, qb = qf[:, lo:lo + QK_NOPE], qf[:, lo + QK_NOPE:lo + HEAD_PAD]
        rq = lax.rsqrt((jnp.sum(qa * qa, axis=-1, keepdims=True) + _sumsq_rope(qb)) * (1.0 / QK_HEAD) + EPS)
        rq = rq * (ATTN_SCALE * LOG2E)
        q_ref[0, :, lo:lo + QK_NOPE] = (qa * gq_n * rq).astype(q_ref.dtype)
        q_ref[0, :, lo + QK_NOPE:lo + HEAD_PAD] = (_rope_half(qb, q_gct, q_gst) * rq).astype(q_ref.dtype)
        ka = kvf[:, lo:lo + QK_NOPE]
        rk = lax.rsqrt((jnp.sum(ka * ka, axis=-1, keepdims=True) + kr_ss) * (1.0 / QK_HEAD) + EPS)
        k_ref[0, :, lo:lo + QK_NOPE] = (ka * gk_n * rk).astype(k_ref.dtype)
        k_ref[0, :, lo + QK_NOPE:lo + HEAD_PAD] = (k_rot * rk).astype(k_ref.dtype)
        v_ref[0, :, h * V_HEAD:(h + 1) * V_HEAD] = kvf[:, lo + QK_NOPE:lo + HEAD_PAD].astype(v_ref.dtype)


def _mla_front(P, wq, wkv, gcq, gckv, gq2, gk2, ct, st):
    B, S, _ = P.shape
    tm = _tile(S, 512)
    HW = MLA_HEADS * HEAD_PAD
    cblk = COL_MLA // MLA_BLOCK
    full = lambda shape: pl.BlockSpec(shape, lambda b, i: (0,) * len(shape))
    return pl.pallas_call(
        _mla_kernel,
        out_shape=(jax.ShapeDtypeStruct((B, S, HW), _ACT),
                   jax.ShapeDtypeStruct((B, S, HW), _ACT),
                   jax.ShapeDtypeStruct((B, S, MLA_WIDTH), _ACT)),
        grid=(B, S // tm),
        in_specs=[pl.BlockSpec((1, tm, MLA_BLOCK), lambda b, i: (b, i, cblk)),
                  full(wq.shape), full(wkv.shape), full(gcq.shape), full(gckv.shape),
                  full(gq2.shape), full(gk2.shape),
                  pl.BlockSpec((tm, LANES), lambda b, i: (i, QK_NOPE // LANES)),
                  pl.BlockSpec((tm, LANES), lambda b, i: (i, QK_NOPE // LANES))],
        out_specs=(pl.BlockSpec((1, tm, HW), lambda b, i: (b, i, 0)),
                   pl.BlockSpec((1, tm, HW), lambda b, i: (b, i, 0)),
                   pl.BlockSpec((1, tm, MLA_WIDTH), lambda b, i: (b, i, 0))),
        compiler_params=_params("parallel", "parallel"),
        name="mla_front",
    )(P, wq, wkv, gcq, gckv, gq2, gk2, ct, st)


def _attn_kernel(q_ref, k_ref, v_ref, o_ref):
    nk = k_ref.shape[1]
    ones = (lax.broadcasted_iota(jnp.int32, (nk, V_HEAD), 1) == 0).astype(v_ref.dtype)
    for h in range(MLA_HEADS):
        q = q_ref[0, :, h * HEAD_PAD:(h + 1) * HEAD_PAD]
        k = k_ref[0, :, h * HEAD_PAD:(h + 1) * HEAD_PAD]
        v = jnp.concatenate([v_ref[0, :, h * V_HEAD:(h + 1) * V_HEAD], ones], axis=1)
        s = lax.dot_general(q, k, (((1,), (1,)), ((), ())), preferred_element_type=F32)
        p = jnp.exp2(s - jnp.max(s, axis=-1, keepdims=True))
        o = jnp.dot(p.astype(v.dtype), v, preferred_element_type=F32)
        o_ref[0, :, h * V_HEAD:(h + 1) * V_HEAD] = (
            o[:, :V_HEAD] / o[:, V_HEAD:V_HEAD + 1]).astype(o_ref.dtype)


def _attention(q, k, v, n_lat):
    B, S, HW = q.shape
    nc = S - n_lat
    assert n_lat % nc == 0, (n_lat, nc)
    tq = _tile(n_lat, 512)
    a_lat = pl.pallas_call(
        _attn_kernel,
        out_shape=jax.ShapeDtypeStruct((B, n_lat, MLA_WIDTH), _ACT),
        grid=(B, n_lat // tq),
        in_specs=[pl.BlockSpec((1, tq, HW), lambda b, i: (b, i, 0)),
                  pl.BlockSpec((1, S, HW), lambda b, i: (b, 0, 0)),
                  pl.BlockSpec((1, S, MLA_WIDTH), lambda b, i: (b, 0, 0))],
        out_specs=pl.BlockSpec((1, tq, MLA_WIDTH), lambda b, i: (b, i, 0)),
        compiler_params=_params("parallel", "arbitrary"),
        name="attention",
    )(q, k, v)
    cblk = n_lat // nc
    ctx_rows = lambda w: pl.BlockSpec((1, nc, w), lambda b: (b, cblk, 0))
    a_ctx = pl.pallas_call(
        _attn_kernel,
        out_shape=jax.ShapeDtypeStruct((B, nc, MLA_WIDTH), _ACT),
        grid=(B,),
        in_specs=[ctx_rows(HW), ctx_rows(HW), ctx_rows(MLA_WIDTH)],
        out_specs=pl.BlockSpec((1, nc, MLA_WIDTH), lambda b: (b, 0, 0)),
        compiler_params=_params("parallel"),
        name="attention_ctx",
    )(q, k, v)
    return a_lat, a_ctx


def _gelu_tanh(x):
    return 0.5 * x * (1.0 + jnp.tanh(math.sqrt(2.0 / math.pi) * (x + 0.044715 * (x * x * x))))


def _sgu_kernel(p_ref, g_ref, w_ref, b_ref, o_ref, *, tm):
    gp = _gelu_tanh(p_ref[0].astype(F32))
    u = gp[:, :SGU_WIDTH]
    v = gp[:, SGU_WIDTH:]
    v = v * lax.rsqrt(jnp.mean(v * v, axis=-1, keepdims=True) + EPS) * g_ref[...]
    vb = v.astype(_MXU)
    bias = b_ref[...]
    for c in range(tm // SGU_CHUNK):
        rs = slice(c * SGU_CHUNK, (c + 1) * SGU_CHUNK)
        for g in range(SGU_GROUPS):
            cs = slice(g * SGU_GROUP_CH, (g + 1) * SGU_GROUP_CH)
            mix = jnp.dot(w_ref[g], vb[rs, cs], preferred_element_type=F32) + bias[:, cs]
            o_ref[0, rs, cs] = (u[rs, cs] * mix).astype(o_ref.dtype)


def _sgu(P, g_sgu, w_sgu, bias, rows):
    B, S, _ = P.shape
    tm = _tile(math.gcd(rows, S), 512)
    full = lambda shape: pl.BlockSpec(shape, lambda b, i: (0,) * len(shape))
    return pl.pallas_call(
        functools.partial(_sgu_kernel, tm=tm),
        out_shape=jax.ShapeDtypeStruct((B, rows, SGU_WIDTH), _ACT),
        grid=(B, rows // tm),
        in_specs=[pl.BlockSpec((1, tm, 2 * SGU_WIDTH), lambda b, i: (b, i, COL_SGU // (2 * SGU_WIDTH))),
                  full(g_sgu.shape), full(w_sgu.shape), full(bias.shape)],
        out_specs=pl.BlockSpec((1, tm, SGU_WIDTH), lambda b, i: (b, i, 0)),
        compiler_params=_params("parallel", "parallel"),
        name="spatial_gating",
    )(P, g_sgu, w_sgu, bias)


def _window_sum(x, g):
    n, C = x.shape
    z = jnp.zeros((POOL_HALO, C), F32)
    xe = jnp.concatenate([z, x, z], axis=0)
    ne = n + 2 * POOL_HALO
    w = xe + pltpu.roll(xe, 1, axis=0)
    for s in range(g):
        k = 1 << s
        w = pltpu.roll(w, k, axis=0) + pltpu.roll(w, ne - k, axis=0)
    return w[POOL_HALO:POOL_HALO + n]


def _pool_kernel(p_ref, w_ref, s_ref, o_ref, *, n_lat):
    rows = o_ref.shape[1]
    for lo, n in ((0, n_lat), (n_lat, rows - n_lat)):
        if n == 0:
            continue
        t = lax.broadcasted_iota(jnp.int32, (n, 1), 0)
        for g, win in enumerate(POOL_WINDOWS):
            cs = slice(g * POOL_GROUP_CH, (g + 1) * POOL_GROUP_CH)
            x = p_ref[0, lo:lo + n, cs].astype(F32)
            half = win // 2
            cnt = (jnp.minimum(t + half, n) - jnp.maximum(t - half, 0)).astype(F32)
            m = _window_sum(x, g) / cnt - x
            y = _dot(m, w_ref[g]) * s_ref[:, cs]
            o_ref[0, lo:lo + n, cs] = y.astype(o_ref.dtype)


def _pool(P, w_pool, s_pool, n_lat, rows):
    B, S, _ = P.shape
    full = lambda shape: pl.BlockSpec(shape, lambda b: (0,) * len(shape))
    return pl.pallas_call(
        functools.partial(_pool_kernel, n_lat=n_lat),
        out_shape=jax.ShapeDtypeStruct((B, rows, POOL_WIDTH), _ACT),
        grid=(B,),
        in_specs=[pl.BlockSpec((1, S, POOL_WIDTH), lambda b: (b, 0, COL_POOL // POOL_WIDTH)),
                  full(w_pool.shape), full(s_pool.shape)],
        out_specs=pl.BlockSpec((1, rows, POOL_WIDTH), lambda b: (b, 0, 0)),
        compiler_params=_params("parallel"),
        name="multiscale_pool",
    )(P, w_pool, s_pool)


def _merge_kernel(al_ref, ac_ref, b_ref, c_ref, ga_ref, gb_ref, gc_ref, x_ref, g1b, g1c,
                  wa_ref, wb_ref, wc_ref, wo_ref, o_ref, *, n_lat, tm):
    def gate(ref):
        return jax.nn.sigmoid(ref[0].astype(F32))
    a = jnp.where(pl.program_id(1) * tm < n_lat, al_ref[0], ac_ref[0])
    y = gate(ga_ref) * jnp.dot(a, wa_ref[...], preferred_element_type=F32)
    y = y + gate(gb_ref) * jnp.dot(b_ref[0], wb_ref[...], preferred_element_type=F32)
    y = y + gate(gc_ref) * jnp.dot(c_ref[0], wc_ref[...], preferred_element_type=F32)
    out = _dot(y, wo_ref[...])
    rows = pl.program_id(1) * tm + lax.broadcasted_iota(jnp.int32, (tm, 1), 0)
    g1 = jnp.where(rows < n_lat, g1b[0], g1c[0])
    o_ref[0] = x_ref[0] + g1 * out


def _merge(a_lat, a_ctx, b, cp, P, xs, mod, wa, wb, wc, wo, base, n_lat, rows):
    B, S, D = xs.shape
    tm = _tile(math.gcd(n_lat, S - n_lat), 256)
    gblk = COL_GATE // D
    nl = n_lat // tm
    tok = lambda w: pl.BlockSpec((1, tm, w), lambda bb, i: (bb, i, 0))
    att_lat = pl.BlockSpec((1, tm, MLA_WIDTH), lambda bb, i: (bb, jnp.minimum(i, nl - 1), 0))
    att_ctx = pl.BlockSpec((1, tm, MLA_WIDTH), lambda bb, i: (bb, jnp.maximum(i - nl, 0), 0))
    gate = lambda k: pl.BlockSpec((1, tm, D), lambda bb, i: (bb, i, gblk + k))
    wres = lambda shape: pl.BlockSpec(shape, lambda bb, i: (0, 0), pipeline_mode=pl.Buffered(1))
    return pl.pallas_call(
        functools.partial(_merge_kernel, n_lat=n_lat, tm=tm),
        out_shape=jax.ShapeDtypeStruct((B, rows, D), F32),
        grid=(B, rows // tm),
        in_specs=[att_lat, att_ctx, tok(SGU_WIDTH), tok(POOL_WIDTH), gate(0), gate(1), gate(2), tok(D)]
        + _mod_specs(D, base, B, 2)
        + [wres(wa.shape), wres(wb.shape), wres(wc.shape), wres(wo.shape)],
        out_specs=tok(D),
        compiler_params=_params("parallel", "parallel"),
        name="merge_out",
    )(a_lat, a_ctx, b, cp, P, P, P, xs, mod, mod, wa, wb, wc, wo)


def _pack_rows(o_ref, y):
    rows, half = y.shape[0], y.shape[1] // 2
    nq = half // LANES
    lo = lax.bitcast_convert_type(y[:, :half].astype(jnp.bfloat16).astype(F32), U32)
    hi = lax.bitcast_convert_type(y[:, half:].astype(jnp.bfloat16).astype(F32), U32)
    w = hi | (lo >> 16)
    for q in range(nq):
        o_ref[pl.ds(q, rows, stride=nq), :] = w[:, q * LANES:(q + 1) * LANES]


def _unpack_rows(ref, q, rows, nq):
    w = ref[pl.ds(q, rows, stride=nq), :]
    lo = lax.bitcast_convert_type(w << 16, F32)
    hi = lax.bitcast_convert_type(w & jnp.uint32(0xFFFF0000), F32)
    return lo, hi


def _router_kernel(x_ref, shb, shc, scb, scc, w_ref, b_ref, h_ref, r_ref, cnt_ref, *, n_lat, tm):
    first = (pl.program_id(0) == 0) & (pl.program_id(1) == 0)

    @pl.when(first)
    def _():
        cnt_ref[...] = jnp.zeros_like(cnt_ref)

    h = _norm_mod(x_ref[0], pl.program_id(1) * tm, n_lat, shb[0], scb[0], shc[0], scc[0])
    _pack_rows(h_ref, h)
    h_hi = h.astype(jnp.bfloat16)
    h_lo = (h - h_hi.astype(F32)).astype(jnp.bfloat16)
    t = jnp.dot(h_hi, w_ref[...], preferred_element_type=F32)
    logits = (t[:, :ROUTE_COLS] + t[:, ROUTE_COLS:]
              + jnp.dot(h_lo, w_ref[:, :ROUTE_COLS], preferred_element_type=F32) + b_ref[...])
    lane = lax.broadcasted_iota(jnp.int32, (tm, ROUTE_COLS), 1)
    ninf = -jnp.inf
    lg = jnp.where(lane < N_GROUPS, logits, ninf)
    mg = jnp.max(lg, axis=-1, keepdims=True)
    g_w = 1.0 / jnp.sum(jnp.exp(lg - mg), axis=-1, keepdims=True)
    g_i = jnp.min(jnp.where(lg == mg, lane, ROUTE_COLS), axis=-1, keepdims=True)
    lo = N_GROUPS + g_i * EXPERTS_PER_GROUP
    le = jnp.where((lane >= lo) & (lane < lo + EXPERTS_PER_GROUP), logits, ninf)
    m1 = jnp.max(le, axis=-1, keepdims=True)
    i1 = jnp.min(jnp.where(le == m1, lane, ROUTE_COLS), axis=-1, keepdims=True)
    le2 = jnp.where(lane == i1, ninf, le)
    m2 = jnp.max(le2, axis=-1, keepdims=True)
    i2 = jnp.min(jnp.where(le2 == m2, lane, ROUTE_COLS), axis=-1, keepdims=True)
    e21 = jnp.exp(m2 - m1)
    w1 = 1.0 / (1.0 + e21)
    w2 = e21 * w1
    onehot = jnp.where((lane == i1) | (lane == i2), 1.0, 0.0)
    ti = lax.broadcasted_iota(jnp.int32, (tm, tm), 0)
    tj = lax.broadcasted_iota(jnp.int32, (tm, tm), 1)
    tri = jnp.where(ti > tj, 1.0, 0.0).astype(jnp.bfloat16)
    before = jnp.dot(tri, onehot.astype(jnp.bfloat16), preferred_element_type=F32) + cnt_ref[...]
    r1 = jnp.sum(jnp.where(lane == i1, before, 0.0), axis=-1, keepdims=True)
    r2 = jnp.sum(jnp.where(lane == i2, before, 0.0), axis=-1, keepdims=True)
    cnt_ref[...] += jnp.sum(onehot, axis=0, keepdims=True)
    out = jnp.where(lane == 0, (i1 - N_GROUPS).astype(F32), 0.0)
    out = jnp.where(lane == 1, (i2 - N_GROUPS).astype(F32), out)
    out = jnp.where(lane == 2, g_w * w1, out)
    out = jnp.where(lane == 3, g_w * w2, out)
    out = jnp.where(lane == 4, r1, out)
    out = jnp.where(lane == 5, r2, out)
    r_ref[0] = out


def _router(xs, mod, w_r, b_r, base, n_lat):
    B, S, D = xs.shape
    tm = _tile(S, 512)
    Q2 = D // (2 * LANES)
    nt = S // tm
    full = lambda shape: pl.BlockSpec(shape, lambda b, i: (0,) * len(shape))
    return pl.pallas_call(
        functools.partial(_router_kernel, n_lat=n_lat, tm=tm),
        out_shape=(jax.ShapeDtypeStruct((B * S * Q2, LANES), U32),
                   jax.ShapeDtypeStruct((B, S, ROUTE_COLS), F32),
                   jax.ShapeDtypeStruct((1, ROUTE_COLS), F32)),
        grid=(B, S // tm),
        in_specs=[pl.BlockSpec((1, tm, D), lambda b, i: (b, i, 0))]
        + _mod_specs(D, base, B, 3) + _mod_specs(D, base, B, 4)
        + [full(w_r.shape), full(b_r.shape)],
        out_specs=(pl.BlockSpec((tm * Q2, LANES), lambda b, i: (b * nt + i, 0)),
                   pl.BlockSpec((1, tm, ROUTE_COLS), lambda b, i: (b, i, 0)),
                   pl.BlockSpec((1, ROUTE_COLS), lambda b, i: (0, 0))),
        compiler_params=_params("arbitrary", "arbitrary"),
        name="moe_router",
    )(xs, mod, mod, mod, mod, w_r, b_r)


def _dispatch_kernel(pos_ref, zlo_ref, h_ref, o_ref, zero_ref, stage, ld_sem, sc_sem, zsem, *, td, nq):
    c = pl.program_id(0)
    n = pl.num_programs(0)
    rows = td * nq
    slot = c % DISPATCH_SLOTS

    def load(chunk, s):
        src = h_ref.at[pl.ds(pl.multiple_of(chunk * rows, rows), rows)]
        return pltpu.make_async_copy(src, stage.at[s], ld_sem.at[s])

    def row_copies_of(s):
        return pltpu.make_async_copy(h_ref.at[pl.ds(0, 2 * rows)], o_ref.at[pl.ds(0, 2 * rows)], sc_sem.at[s])

    def zero_fill(e):
        start = pl.multiple_of(zlo_ref[e], nq)
        return pltpu.make_async_copy(zero_ref, o_ref.at[pl.ds(start, MOE_TILE * nq)], zsem)

    @pl.when(c == 0)
    def _():
        zero_ref[...] = jnp.zeros_like(zero_ref)
        for e in range(2 * N_EXPERTS):
            @pl.when(zlo_ref[e] >= 0)
            def _():
                zero_fill(e).start()
        for e in range(2 * N_EXPERTS):
            @pl.when(zlo_ref[e] >= 0)
            def _():
                zero_fill(e).wait()

        load(0, 0).start()

    @pl.when(c >= 2)
    def _():
        row_copies_of((c + 1) % DISPATCH_SLOTS).wait()

    @pl.when(c + 1 < n)
    def _():
        load(c + 1, (c + 1) % DISPATCH_SLOTS).start()

    load(c, slot).wait()

    def issue(r, carry):
        src = stage.at[slot, pl.ds(pl.multiple_of(r * nq, nq), nq)]
        for k in range(2):
            dst = o_ref.at[pl.ds(pl.multiple_of(pos_ref[2 * (c * td + r) + k], nq), nq)]
            pltpu.make_async_copy(src, dst, sc_sem.at[slot]).start(priority=k)
        return carry

    lax.fori_loop(0, td, issue, 0)

    @pl.when(c == n - 1)
    def _():
        @pl.when(c >= 1)
        def _():
            row_copies_of((c + DISPATCH_SLOTS - 1) % DISPATCH_SLOTS).wait()
        row_copies_of(slot).wait()


DISPATCH_SLOTS = 3


def _dispatch(pos, zlo, h2p, T, n_rows):
    nq = h2p.shape[0] // T
    td = _tile(T // 2, 512, SUBLANES)
    return pl.pallas_call(
        functools.partial(_dispatch_kernel, td=td, nq=nq),
        out_shape=jax.ShapeDtypeStruct((n_rows * nq, LANES), h2p.dtype),
        grid_spec=pltpu.PrefetchScalarGridSpec(
            num_scalar_prefetch=2,
            grid=(T // td,),
            in_specs=[pl.BlockSpec(memory_space=pl.ANY)],
            out_specs=pl.BlockSpec(memory_space=pl.ANY),
            scratch_shapes=[pltpu.VMEM((MOE_TILE * nq, LANES), h2p.dtype),
                            pltpu.VMEM((DISPATCH_SLOTS, td * nq, LANES), h2p.dtype),
                            pltpu.SemaphoreType.DMA((DISPATCH_SLOTS,)),
                            pltpu.SemaphoreType.DMA((DISPATCH_SLOTS,)),
                            pltpu.SemaphoreType.DMA(())],
        ),
        compiler_params=pltpu.CompilerParams(dimension_semantics=("arbitrary",),
                                             vmem_limit_bytes=VMEM_LIMIT, has_side_effects=True),
        name="moe_dispatch",
    )(pos, zlo, h2p)


def _expert_kernel(te_ref, nv_ref, x_ref, wg_ref, wu_ref, wd_ref, o_ref, wgu_s, wd_s, *, nq):
    j = pl.program_id(0)
    used = j < nv_ref[0]
    new_expert = (j == 0) | (te_ref[j] != te_ref[jnp.maximum(j - 1, 0)])

    @pl.when(used & new_expert)
    def _():
        wgu_s[:, :EXPERT_HIDDEN] = wg_ref[0, 0].astype(wgu_s.dtype)
        wgu_s[:, EXPERT_HIDDEN:] = wu_ref[0, 0].astype(wgu_s.dtype)
        wd_s[...] = wd_ref[0, 0].astype(wd_s.dtype)

    @pl.when(used)
    def _():
        los, his = [], []
        for q in range(nq):
            lo, hi = _unpack_rows(x_ref, q, MOE_TILE, nq)
            los.append(lo.astype(_MXU))
            his.append(hi.astype(_MXU))
        x = jnp.concatenate(los + his, axis=1)
        hgu = jnp.dot(x, wgu_s[...], preferred_element_type=F32)
        hg = hgu[:, :EXPERT_HIDDEN]
        hid = hg * jax.nn.sigmoid(hg) * hgu[:, EXPERT_HIDDEN:]
        _pack_rows(o_ref, _dot(hid, wd_s[...]))

    @pl.when(jnp.logical_not(used))
    def _():
        o_ref[...] = jnp.zeros_like(o_ref)


def _experts(tile_e, nvalid, xsorted, wg, wu, wd, layer):
    _, E, D, Hd = wg.shape
    nq = D // (2 * LANES)
    NT = xsorted.shape[0] // (MOE_TILE * nq)
    return pl.pallas_call(
        functools.partial(_expert_kernel, nq=nq),
        out_shape=jax.ShapeDtypeStruct(xsorted.shape, U32),
        grid_spec=pltpu.PrefetchScalarGridSpec(
            num_scalar_prefetch=2,
            grid=(NT,),
            in_specs=[pl.BlockSpec((MOE_TILE * nq, LANES), lambda j, te, nv: (j, 0)),
                      pl.BlockSpec((1, 1, D, Hd), lambda j, te, nv: (layer, te[j], 0, 0)),
                      pl.BlockSpec((1, 1, D, Hd), lambda j, te, nv: (layer, te[j], 0, 0)),
                      pl.BlockSpec((1, 1, Hd, D), lambda j, te, nv: (layer, te[j], 0, 0))],
            out_specs=pl.BlockSpec((MOE_TILE * nq, LANES), lambda j, te, nv: (j, 0)),
            scratch_shapes=[pltpu.VMEM((D, 2 * Hd), _MXU), pltpu.VMEM((Hd, D), _MXU)],
        ),
        compiler_params=_params("arbitrary"),
        name="moe_experts",
    )(tile_e, nvalid, xsorted, wg, wu, wd)


def _combine_kernel(pos_ref, y_ref, x_ref, r_ref, g2b, g2c, o_ref, a0, a1, b0, b1, sem_a, sem_b,
                    *, n_lat, tc, S, nq):
    b = pl.program_id(0)
    i = pl.program_id(1)
    nt = pl.num_programs(1)
    step = b * nt + i
    last = pl.num_programs(0) * nt - 1
    sets = ((a0, a1, sem_a), (b0, b1, sem_b))

    def start_gathers(base, bufs):
        buf0, buf1, sem = bufs

        def issue(r, carry):
            dst = pl.ds(pl.multiple_of(r * nq, nq), nq)
            for k, buf in ((0, buf0), (1, buf1)):
                src = y_ref.at[pl.ds(pl.multiple_of(pos_ref[2 * (base + r) + k], nq), nq)]
                pltpu.make_async_copy(src, buf.at[dst], sem).start(priority=k)
            return carry

        lax.fori_loop(0, tc, issue, 0)

    def consume(bufs):
        buf0, buf1, sem = bufs
        for buf in (buf0, buf1):
            pltpu.make_async_copy(y_ref.at[pl.ds(0, tc * nq)], buf, sem).wait()
        r = r_ref[0]
        c1, c2 = r[:, 2:3], r[:, 3:4]
        rows = i * tc + lax.broadcasted_iota(jnp.int32, (tc, 1), 0)
        lat = rows < n_lat
        half = x_ref.shape[2] // 2
        for q in range(nq):
            lo0, hi0 = _unpack_rows(buf0, q, tc, nq)
            lo1, hi1 = _unpack_rows(buf1, q, tc, nq)
            for col, y0, y1 in ((q * LANES, lo0, lo1), (half + q * LANES, hi0, hi1)):
                cs = slice(col, col + LANES)
                g2 = jnp.where(lat, g2b[0, :, cs], g2c[0, :, cs])
                o_ref[0, :, cs] = x_ref[0, :, cs] + g2 * (c1 * y0 + c2 * y1)

    @pl.when(step == 0)
    def _():
        start_gathers(0, sets[0])

    next_base = jnp.where(i + 1 < nt, b * S + (i + 1) * tc, (b + 1) * S)
    for parity in range(2):
        @pl.when((step < last) & (step % 2 == parity))
        def _():
            start_gathers(next_base, sets[1 - parity])

    for parity in range(2):
        @pl.when(step % 2 == parity)
        def _():
            consume(sets[parity])


def _combine(pos, ysorted, xs, route, mod, base, n_lat, rows):
    B, S, D = xs.shape
    nq = D // (2 * LANES)
    tc = _tile(math.gcd(rows, S), 256, SUBLANES)
    return pl.pallas_call(
        functools.partial(_combine_kernel, n_lat=n_lat, tc=tc, S=S, nq=nq),
        out_shape=jax.ShapeDtypeStruct((B, rows, D), F32),
        grid_spec=pltpu.PrefetchScalarGridSpec(
            num_scalar_prefetch=1,
            grid=(B, rows // tc),
            in_specs=[pl.BlockSpec(memory_space=pl.ANY),
                      pl.BlockSpec((1, tc, D), lambda b, i, pos: (b, i, 0)),
                      pl.BlockSpec((1, tc, ROUTE_COLS), lambda b, i, pos: (b, i, 0))]
            + _mod_specs(D, base, B, 5),
            out_specs=pl.BlockSpec((1, tc, D), lambda b, i, pos: (b, i, 0)),
            scratch_shapes=[pltpu.VMEM((tc * nq, LANES), U32) for _ in range(4)]
            + [pltpu.SemaphoreType.DMA(()), pltpu.SemaphoreType.DMA(())],
        ),
        compiler_params=_params("arbitrary", "arbitrary"),
        name="moe_combine",
    )(pos, ysorted, xs, route, mod, mod)


def _swap_perm():
    j = np.arange(QK_ROPE)
    a, h, f = j // ROPE_AXIS_DIM, (j // ROPE_FREQS) % 2, j % ROPE_FREQS
    return a * ROPE_AXIS_DIM + (1 - h) * ROPE_FREQS + f


def _rope_tables(n_lat, S):
    t = jnp.arange(n_lat)
    row = (t // GRID_W).astype(F32)
    col = (t % GRID_W).astype(F32)
    inv = ROPE_THETA ** (-jnp.arange(0, ROPE_AXIS_DIM, 2, dtype=F32) / ROPE_AXIS_DIM)
    a0, a1 = row[:, None] * inv, col[:, None] * inv
    c64 = jnp.concatenate([jnp.cos(a0), jnp.cos(a0), jnp.cos(a1), jnp.cos(a1)], axis=1)
    s64 = jnp.concatenate([-jnp.sin(a0), jnp.sin(a0), -jnp.sin(a1), jnp.sin(a1)], axis=1)
    nc = S - n_lat
    c64 = jnp.concatenate([c64, jnp.ones((nc, QK_ROPE), F32)], axis=0)
    s64 = jnp.concatenate([s64, jnp.zeros((nc, QK_ROPE), F32)], axis=0)
    ct = jnp.concatenate([jnp.ones((S, QK_NOPE), F32), c64, jnp.zeros((S, QK_ROPE), F32)], axis=1)
    st = jnp.concatenate([jnp.zeros((S, QK_NOPE), F32), s64, jnp.zeros((S, QK_ROPE), F32)], axis=1)
    return ct, st


def _head_gains(g, perm):
    z = jnp.zeros((QK_ROPE,), F32)
    ga = jnp.concatenate([g[:QK_NOPE], g[QK_NOPE:], z])
    gb = jnp.concatenate([jnp.zeros((QK_NOPE,), F32), g[QK_NOPE:][perm], z])
    return jnp.stack([ga, gb], axis=0)


def _prep_w_in(w, D, perm):
    o = np.cumsum((0, Q_LORA, KV_LORA, QK_ROPE, 2 * SGU_WIDTH, POOL_WIDTH, 3 * D))
    wq, wkv, wkr, wsgu, wpool, wgate = (w[:, :, o[k]:o[k + 1]].astype(_MXU) for k in range(6))
    pad = jnp.zeros(w.shape[:2] + (MLA_BLOCK - Q_LORA - KV_LORA - 2 * QK_ROPE,), _MXU)
    return jnp.concatenate([wsgu, wpool, wq, wkv, pad, wkr, wkr[:, :, perm], wgate], axis=2)


def _slots(route, counts, T, nq):
    e = route[:, 0:2].astype(jnp.int32)
    rank = route[:, 4:6].astype(jnp.int32)
    cnt = counts[0, N_GROUPS:N_GROUPS + N_EXPERTS].astype(jnp.int32)
    padded = (cnt + MOE_TILE - 1) // MOE_TILE * MOE_TILE
    end = jnp.cumsum(padded)
    off = end - padded
    own = e[:, :, None] == jnp.arange(N_EXPERTS, dtype=jnp.int32)
    pos = rank + jnp.sum(jnp.where(own, off, 0), axis=-1)
    n_rows = 2 * T + N_EXPERTS * MOE_TILE
    NT = n_rows // MOE_TILE
    nvalid = (end[-1] // MOE_TILE).astype(jnp.int32)
    tiles = jnp.minimum(jnp.arange(NT, dtype=jnp.int32), nvalid - 1)
    tile_e = jnp.sum((end[None, :] <= (tiles * MOE_TILE)[:, None]).astype(jnp.int32), axis=1)
    tail = nvalid + jnp.arange(N_EXPERTS, dtype=jnp.int32)
    zlo = jnp.concatenate([jnp.where(cnt > 0, (end - MOE_TILE) * nq, -1),
                           jnp.where(tail < NT, tail * (MOE_TILE * nq), -1)]).astype(jnp.int32)
    return (pos.reshape(-1) * nq).astype(jnp.int32), zlo, tile_e, nvalid.reshape(1), n_rows


def kernel(x, c, ctx, c_ctx, w_ada, b_ada, w_in, g_cq, g_ckv, w_uq, w_ukv, g_q, g_k, w_sgu, b_sgu, g_sgu,
           w_pool, s_pool, w_ao, w_bo, w_co, w_out, w_rg, b_rg, w_re, b_re, w_e_gate, w_e_up, w_e_down):
    B, n, D = x.shape
    nc = ctx.shape[1]
    S = n + nc
    T = B * S
    L = w_ada.shape[0]
    perm = _swap_perm()

    R = -(-(B + 1) // SUBLANES) * SUBLANES
    cc = jnp.concatenate([c, c_ctx[None], jnp.zeros((R - B - 1, D), F32)], axis=0)
    mod = _ada(cc, w_ada, b_ada)
    mod2 = mod.reshape(L * R, 1, 6 * D)

    ct, st = _rope_tables(n, S)
    w_in_r = _prep_w_in(w_in, D, perm)
    xs = jnp.concatenate([x, ctx], axis=1)

    for i in range(L):
        base = i * R
        wq = w_uq[i].reshape(Q_LORA, MLA_HEADS, QK_HEAD)
        wq = jnp.concatenate([wq, wq[:, :, QK_NOPE:][:, :, perm]], axis=2).reshape(Q_LORA, -1).astype(_MXU)
        wkv = w_ukv[i].astype(_MXU)
        bias = jnp.repeat(b_sgu[i].T, SGU_GROUP_CH, axis=1)
        w_r = jnp.concatenate([w_rg[i], jnp.moveaxis(w_re[i], 0, 1).reshape(D, N_EXPERTS),
                               jnp.zeros((D, ROUTE_COLS - N_GROUPS - N_EXPERTS), F32)], axis=1)
        b_r = jnp.concatenate([b_rg[i], b_re[i].reshape(-1),
                               jnp.zeros((ROUTE_COLS - N_GROUPS - N_EXPERTS,), F32)])[None]
        w_r_hi = w_r.astype(jnp.bfloat16)
        w_r = jnp.concatenate([w_r_hi, (w_r - w_r_hi.astype(F32)).astype(jnp.bfloat16)], axis=1)

        P = _inproj(xs, mod2, w_in_r, i, base, n)
        q, k, v = _mla_front(P, wq, wkv, g_cq[i][None], g_ckv[i][None],
                             _head_gains(g_q[i], perm), _head_gains(g_k[i], perm), ct, st)
        a_lat, a_ctx = _attention(q, k, v, n)
        rows = S if i < L - 1 else n
        bb = _sgu(P, g_sgu[i][None], w_sgu[i].astype(_MXU), bias, rows)
        cp = _pool(P, w_pool[i].astype(_MXU), s_pool[i][None], n, rows)
        xs = _merge(a_lat, a_ctx, bb, cp, P, xs, mod2, w_ao[i].astype(_MXU), w_bo[i].astype(_MXU),
                    w_co[i].astype(_MXU), w_out[i].astype(_MXU), base, n, rows)

        Tm = B * rows
        h2, route, counts = _router(xs, mod2, w_r, b_r, base, n)
        pos, zlo, tile_e, nvalid, n_rows = _slots(route.reshape(Tm, ROUTE_COLS), counts, Tm, D // (2 * LANES))
        xsorted = _dispatch(pos, zlo, h2, Tm, n_rows)
        ysorted = _experts(tile_e, nvalid, xsorted, w_e_gate, w_e_up, w_e_down, i)
        xs = _combine(pos, ysorted, xs, route, mod2, base, n, rows)

    return xs
```
